```python
import math
import jax, jax.numpy as jnp
from jax import lax
import numpy as np

D_MODEL = 1024
BATCH = 8
SEQ = 2048
DEPTH = 2
DEC_BATCH = 128
DEC_SEQ = 1
PAST_LEN = 16384
PAGE_SIZE = 128

N_MIXERS = 2
SSD_EXPAND = 2
SSD_INNER = SSD_EXPAND * D_MODEL
SSD_HEAD_DIM = 64
SSD_HEADS = SSD_INNER // SSD_HEAD_DIM
SSD_GROUPS = 8
SSD_HPG = SSD_HEADS // SSD_GROUPS
SSD_STATE = 128
SSD_CONV = 4
SSD_CHUNK = 128
SSD_GN = SSD_GROUPS * SSD_STATE
SSD_CONV_DIM = SSD_INNER + 2 * SSD_GN
SSD_IN_DIM = SSD_INNER + SSD_CONV_DIM + SSD_HEADS
SC_WIDTH = 3
N_EXPERTS = 32
TOP_K = 4
D_EXPERT = D_MODEL
SWIGLU_LIMIT = 7.0
SWIGLU_ALPHA = 1.702
EPS = 1e-5

kernel_name = 'hybrid_ssd_shortconv_moe_adaln_step'


def rmsnorm(x, g):
    xf = x.astype(jnp.float32)
    y = xf * lax.rsqrt(jnp.mean(xf * xf, axis=-1, keepdims=True) + EPS)
    return y.astype(x.dtype) * g


def causal_dwconv(u, buf, w):
    full = jnp.concatenate([buf.astype(u.dtype), u], axis=1)
    l = u.shape[1]
    k_w = w.shape[0]
    out = full[:, 0:l] * w[0]
    for k in range(1, k_w):
        out = out + full[:, k:k + l] * w[k]
    return out, full[:, l:]


def ssd_scan(x, dt, A, Bm, Cm, h0):
    out_dtype = x.dtype
    b, l = x.shape[0], x.shape[1]
    q = min(SSD_CHUNK, l)
    nc = -(-l // q)
    pad = nc * q - l
    x = x.astype(jnp.float32)
    Bm = Bm.astype(jnp.float32)
    Cm = Cm.astype(jnp.float32)
    dt = dt.astype(jnp.float32)
    if pad:
        pw = ((0, 0), (0, pad), (0, 0), (0, 0))
        x = jnp.pad(x, pw)
        Bm = jnp.pad(Bm, pw)
        Cm = jnp.pad(Cm, pw)
        dt = jnp.pad(dt, ((0, 0), (0, pad), (0, 0)))
    xg = x.reshape(b, nc, q, SSD_GROUPS, SSD_HPG, SSD_HEAD_DIM)
    dtc = dt.reshape(b, nc, q, SSD_GROUPS, SSD_HPG)
    Bc = Bm.reshape(b, nc, q, SSD_GROUPS, SSD_STATE)
    Cc = Cm.reshape(b, nc, q, SSD_GROUPS, SSD_STATE)
    dA = dtc * A.reshape(SSD_GROUPS, SSD_HPG)
    Acs = jnp.cumsum(dA, axis=2)
    xdt = xg * dtc[..., None]
    diff = Acs[:, :, :, None] - Acs[:, :, None, :]
    mask = jnp.tril(jnp.ones((q, q), dtype=bool))[:, :, None, None]
    Lmat = jnp.exp(jnp.where(mask, diff, -jnp.inf))
    CB = jnp.einsum('bclgn,bcsgn->bclsg', Cc, Bc)
    y_diag = jnp.einsum('bclsg,bclsgr,bcsgrp->bclgrp', CB, Lmat, xdt)
    decay = jnp.exp(Acs[:, :, -1:] - Acs)
    st = jnp.einsum('bcsgn,bcsgr,bcsgrp->bcgrpn', Bc, decay, xdt)
    chunk_decay = jnp.exp(Acs[:, :, -1])

    def step(h, inp):
        dec, s = inp
        return dec[..., None, None] * h + s, h

    h_init = h0.astype(jnp.float32).reshape(b, SSD_GROUPS, SSD_HPG, SSD_HEAD_DIM, SSD_STATE)
    h_final, prev = lax.scan(step, h_init, (jnp.moveaxis(chunk_decay, 1, 0), jnp.moveaxis(st, 1, 0)))
    prev = jnp.moveaxis(prev, 0, 1)
    y_off = jnp.einsum('bclgn,bcgrpn,bclgr->bclgrp', Cc, prev, jnp.exp(Acs))
    y = (y_diag + y_off).reshape(b, nc * q, SSD_HEADS, SSD_HEAD_DIM)[:, :l]
    return y.astype(out_dtype), h_final.reshape(b, SSD_HEADS, SSD_HEAD_DIM, SSD_STATE)


def ssd_mixer(h, conv_buf, ssm_state, p):
    b, l, _ = h.shape
    zxbcdt = h @ p['ssd_w_in']
    z = zxbcdt[..., :SSD_INNER]
    xbc = zxbcdt[..., SSD_INNER:SSD_INNER + SSD_CONV_DIM]
    dt_raw = zxbcdt[..., SSD_INNER + SSD_CONV_DIM:]
    xbc_c, new_buf = causal_dwconv(xbc, conv_buf, p['ssd_conv_w'])
    xbc_c = jax.nn.silu(xbc_c + p['ssd_conv_b'])
    xs = xbc_c[..., :SSD_INNER].reshape(b, l, SSD_HEADS, SSD_HEAD_DIM)
    Bm = xbc_c[..., SSD_INNER:SSD_INNER + SSD_GN].reshape(b, l, SSD_GROUPS, SSD_STATE)
    Cm = xbc_c[..., SSD_INNER + SSD_GN:].reshape(b, l, SSD_GROUPS, SSD_STATE)
    dt = jax.nn.softplus(dt_raw.astype(jnp.float32) + p['ssd_dt_bias'].astype(jnp.float32))
    A = -jnp.exp(p['ssd_A_log'].astype(jnp.float32))
    y, new_state = ssd_scan(xs, dt, A, Bm, Cm, ssm_state)
    y = y + xs * p['ssd_D'][:, None]
    g = (y * jax.nn.silu(z.reshape(b, l, SSD_HEADS, SSD_HEAD_DIM))).reshape(b, l, SSD_GROUPS, SSD_INNER // SSD_GROUPS)
    gf = g.astype(jnp.float32)
    gn = gf * lax.rsqrt(jnp.mean(gf * gf, axis=-1, keepdims=True) + EPS)
    gn = gn.reshape(b, l, SSD_INNER).astype(h.dtype) * p['ssd_norm_w']
    return gn @ p['ssd_w_out'], new_buf, new_state.astype(ssm_state.dtype)


def short_conv_mixer(h, buf, p):
    bcx = h @ p['sc_w_in']
    bg = bcx[..., :D_MODEL]
    cg = bcx[..., D_MODEL:2 * D_MODEL]
    xh = bcx[..., 2 * D_MODEL:]
    conv, new_buf = causal_dwconv(cg * xh, buf, p['sc_conv_w'])
    return (bg * conv) @ p['sc_w_out'], new_buf


def moe(h, i, p):
    shp = h.shape
    t = h.reshape(-1, D_MODEL)
    logits = (t @ p['router_w'][i]).astype(jnp.float32) + p['router_b'][i].astype(jnp.float32)
    vals, idx = lax.top_k(logits, TOP_K)
    probs = jax.nn.softmax(vals, axis=-1)
    comb = jnp.einsum('tk,tke->te', probs, jax.nn.one_hot(idx, N_EXPERTS, dtype=jnp.float32)).astype(t.dtype)
    out = jnp.zeros_like(t)
    for e in range(N_EXPERTS):
        gu = t @ p['moe_w_gate_up'][i, e] + p['moe_b_gate_up'][i, e]
        gate = jnp.minimum(gu[:, 0::2], SWIGLU_LIMIT)
        up = jnp.clip(gu[:, 1::2], -SWIGLU_LIMIT, SWIGLU_LIMIT)
        act = (up + 1.0) * (gate * jax.nn.sigmoid(SWIGLU_ALPHA * gate))
        o = act @ p['moe_w_down'][i, e] + p['moe_b_down'][i, e]
        out = out + comb[:, e:e + 1] * o
    return out.reshape(shp)


def trunk(x, c, ssm_conv_buf, ssm_state, sconv_buf, p):
    for i in range(DEPTH):
        mod = jax.nn.silu(c) @ p['ada_w'][i] + p['ada_b'][i]
        sh1, sc1, g1, sh2, sc2, g2 = jnp.split(mod[:, None, :], 6, axis=-1)
        h = rmsnorm(x, p['norm1_w'][i]) * (1.0 + sc1) + sh1
        if i % N_MIXERS == 0:
            m, ssm_conv_buf, ssm_state = ssd_mixer(h, ssm_conv_buf, ssm_state, p)
        else:
            m, sconv_buf = short_conv_mixer(h, sconv_buf, p)
        x = x + g1 * m
        h = rmsnorm(x, p['norm2_w'][i]) * (1.0 + sc2) + sh2
        x = x + g2 * moe(h, i, p)
    return rmsnorm(x, p['final_norm_w']), ssm_state, ssm_conv_buf, sconv_buf


def setup_inputs(seed: int = 0) -> dict:
    key = jax.random.key(seed)
    ks = jax.random.split(key, 32)
    f32 = jnp.float32

    def nrm(k, shape, scale):
        return jax.random.normal(k, shape, f32) * scale

    dt0 = jnp.exp(jax.random.uniform(ks[13], (SSD_HEADS,), f32, math.log(1e-3), math.log(1e-1)))
    return {
        'x_prompt': nrm(ks[0], (BATCH, SEQ, D_MODEL), 1.0),
        'x_sample': nrm(ks[1], (DEC_BATCH, DEC_SEQ, D_MODEL), 1.0),
        'c_prompt': nrm(ks[2], (BATCH, D_MODEL), 1.0),
        'c_sample': nrm(ks[3], (DEC_BATCH, D_MODEL), 1.0),
        'state_ssm': nrm(ks[4], (DEC_BATCH, SSD_HEADS, SSD_HEAD_DIM, SSD_STATE), 0.5),
        'state_ssm_conv': nrm(ks[5], (DEC_BATCH, SSD_CONV - 1, SSD_CONV_DIM), 1.0),
        'state_sconv': nrm(ks[6], (DEC_BATCH, SC_WIDTH - 1, D_MODEL), 1.0),
        'ada_w': nrm(ks[7], (DEPTH, D_MODEL, 6 * D_MODEL), 0.5 * D_MODEL ** -0.5),
        'ada_b': nrm(ks[8], (DEPTH, 6 * D_MODEL), 0.02),
        'norm1_w': 1.0 + nrm(ks[9], (DEPTH, D_MODEL), 0.02),
        'norm2_w': 1.0 + nrm(ks[10], (DEPTH, D_MODEL), 0.02),
        'ssd_w_in': nrm(ks[11], (D_MODEL, SSD_IN_DIM), D_MODEL ** -0.5),
        'ssd_conv_w': nrm(ks[12], (SSD_CONV, SSD_CONV_DIM), SSD_CONV ** -0.5),
        'ssd_conv_b': nrm(ks[14], (SSD_CONV_DIM,), 0.02),
        'ssd_dt_bias': dt0 + jnp.log(-jnp.expm1(-dt0)),
        'ssd_A_log': jnp.log(jax.random.uniform(ks[15], (SSD_HEADS,), f32, 1.0, 16.0)),
        'ssd_D': 1.0 + nrm(ks[16], (SSD_HEADS,), 0.02),
        'ssd_norm_w': 1.0 + nrm(ks[17], (SSD_INNER,), 0.02),
        'ssd_w_out': nrm(ks[18], (SSD_INNER, D_MODEL), SSD_INNER ** -0.5),
        'sc_w_in': nrm(ks[19], (D_MODEL, 3 * D_MODEL), D_MODEL ** -0.5),
        'sc_conv_w': nrm(ks[20], (SC_WIDTH, D_MODEL), SC_WIDTH ** -0.5),
        'sc_w_out': nrm(ks[21], (D_MODEL, D_MODEL), D_MODEL ** -0.5),
        'router_w': nrm(ks[22], (DEPTH, D_MODEL, N_EXPERTS), D_MODEL ** -0.5),
        'router_b': nrm(ks[23], (DEPTH, N_EXPERTS), 0.01),
        'moe_w_gate_up': nrm(ks[24], (DEPTH, N_EXPERTS, D_MODEL, 2 * D_EXPERT), D_MODEL ** -0.5),
        'moe_b_gate_up': nrm(ks[25], (DEPTH, N_EXPERTS, 2 * D_EXPERT), 0.01),
        'moe_w_down': nrm(ks[26], (DEPTH, N_EXPERTS, D_EXPERT, D_MODEL), D_EXPERT ** -0.5),
        'moe_b_down': nrm(ks[27], (DEPTH, N_EXPERTS, D_MODEL), 0.01),
        'final_norm_w': 1.0 + nrm(ks[28], (D_MODEL,), 0.02),
    }


def reference(x_prompt, x_sample, c_prompt, c_sample, state_ssm, state_ssm_conv, state_sconv,
              ada_w, ada_b, norm1_w, norm2_w, ssd_w_in, ssd_conv_w, ssd_conv_b, ssd_dt_bias, ssd_A_log,
              ssd_D, ssd_norm_w, ssd_w_out, sc_w_in, sc_conv_w, sc_w_out, router_w, router_b,
              moe_w_gate_up, moe_b_gate_up, moe_w_down, moe_b_down, final_norm_w):
    p = dict(ada_w=ada_w, ada_b=ada_b, norm1_w=norm1_w, norm2_w=norm2_w, ssd_w_in=ssd_w_in,
             ssd_conv_w=ssd_conv_w, ssd_conv_b=ssd_conv_b, ssd_dt_bias=ssd_dt_bias, ssd_A_log=ssd_A_log,
             ssd_D=ssd_D, ssd_norm_w=ssd_norm_w, ssd_w_out=ssd_w_out, sc_w_in=sc_w_in, sc_conv_w=sc_conv_w,
             sc_w_out=sc_w_out, router_w=router_w, router_b=router_b, moe_w_gate_up=moe_w_gate_up,
             moe_b_gate_up=moe_b_gate_up, moe_w_down=moe_w_down, moe_b_down=moe_b_down,
             final_norm_w=final_norm_w)
    bp = x_prompt.shape[0]
    ssm0 = jnp.zeros((bp, SSD_HEADS, SSD_HEAD_DIM, SSD_STATE), x_prompt.dtype)
    ssm_conv0 = jnp.zeros((bp, SSD_CONV - 1, SSD_CONV_DIM), x_prompt.dtype)
    sconv0 = jnp.zeros((bp, SC_WIDTH - 1, D_MODEL), x_prompt.dtype)
    y_prompt, ssm_p, ssm_conv_p, sconv_p = trunk(x_prompt, c_prompt, ssm_conv0, ssm0, sconv0, p)
    y_sample, ssm_s, ssm_conv_s, sconv_s = trunk(x_sample, c_sample, state_ssm_conv, state_ssm, state_sconv, p)
    return (y_prompt, y_sample, ssm_p, ssm_conv_p, sconv_p, ssm_s, ssm_conv_s, sconv_s)
```

```python
import functools

import jax
import jax.numpy as jnp
from jax import lax
from jax.experimental import pallas as pl
from jax.experimental.pallas import tpu as pltpu

F32 = jnp.float32
BF16 = jnp.bfloat16

D_MODEL = 1024
BATCH = 8
SEQ = 2048
DEC_BATCH = 128
SSD_INNER = 2048
SSD_HEAD_DIM = 64
SSD_HEADS = 32
SSD_GROUPS = 8
SSD_HPG = 4
SSD_STATE = 128
SSD_CONV = 4
SSD_CHUNK = 128
SSD_GN = 1024
SSD_CONV_DIM = 4096
SC_WIDTH = 3
N_EXPERTS = 32
TOP_K = 4
SWIGLU_LIMIT = 7.0
SWIGLU_ALPHA = 1.702
EPS = 1e-5

LANES = 128
HEADS_PAD = LANES
ZX_COLS = SSD_CONV_DIM + SSD_INNER + HEADS_PAD
T_PROMPT = BATCH * SEQ
T_ALL = T_PROMPT + DEC_BATCH
MOE_TM = 256
T_PAD = 16640
N_ASSIGN = TOP_K * T_ALL
MOE_TILES = (N_ASSIGN + N_EXPERTS * (MOE_TM - 1)) // MOE_TM + 1
P_PAD = MOE_TILES * MOE_TM
Y_ROWS = TOP_K * T_PAD + MOE_TM
VMEM_LIMIT = 48 * 1024 * 1024


def _cp(sem, vmem=VMEM_LIMIT):
    return pltpu.CompilerParams(dimension_semantics=sem, vmem_limit_bytes=vmem)


def _sigmoid(x):
    return 1.0 / (1.0 + jnp.exp(-x))


def _silu(x):
    return x * _sigmoid(x)


def _softplus(x):
    return jnp.maximum(x, 0.0) + jnp.log1p(jnp.exp(-jnp.abs(x)))


def _normmod(x, nw, sc, sh):
    y = x * lax.rsqrt(jnp.mean(x * x, axis=-1, keepdims=True) + EPS)
    return y * nw * (1.0 + sc) + sh


def _split3(x):
    hi = x.astype(BF16)
    r1 = x - hi.astype(F32)
    mid = r1.astype(BF16)
    lo = (r1 - mid.astype(F32)).astype(BF16)
    return hi, mid, lo


def _dot_exact_lhs(x, m01):
    hi, mid, lo = _split3(x)
    d = functools.partial(jnp.dot, preferred_element_type=F32)
    return d(hi, m01) + d(mid, m01) + d(lo, m01)


def _dot_exact_rhs(m01, x):
    hi, mid, lo = _split3(x)
    d = functools.partial(jnp.dot, preferred_element_type=F32)
    return d(m01, hi) + d(m01, mid) + d(m01, lo)


def _dot_nt(a, b):
    return lax.dot_general(a, b, (((1,), (1,)), ((), ())), preferred_element_type=F32)


def _ada_body(c_ref, w_ref, b_ref, o_ref):
    s = _silu(c_ref[...]).astype(BF16)
    o_ref[...] = jnp.dot(s, w_ref[...].astype(BF16), preferred_element_type=F32) + b_ref[...]


def _ada_call(c_all, ada_w, ada_b):
    n = c_all.shape[0]
    depth = ada_w.shape[0]
    return pl.pallas_call(
        _ada_body,
        grid=(depth, 6),
        in_specs=[
            pl.BlockSpec((n, D_MODEL), lambda l, j: (0, 0)),
            pl.BlockSpec((None, D_MODEL, D_MODEL), lambda l, j: (l, 0, j)),
            pl.BlockSpec((None, 1, D_MODEL), lambda l, j: (l, 0, j)),
        ],
        out_specs=pl.BlockSpec((None, n, D_MODEL), lambda l, j: (l, 0, j)),
        out_shape=jax.ShapeDtypeStruct((depth, n, 6 * D_MODEL), F32),
        compiler_params=_cp(("arbitrary", "arbitrary")),
        name="ada_mod",
    )(c_all, ada_w, ada_b.reshape(depth, 1, 6 * D_MODEL))


def _col_chunks(n, step=512):
    return [(s, min(step, n - s)) for s in range(0, n, step)]


def _nm_body(x_ref, nw_ref, sc_ref, sh_ref, w_ref, o_ref):
    h = _normmod(x_ref[...], nw_ref[...], sc_ref[...], sh_ref[...]).astype(BF16)
    for s, n in _col_chunks(o_ref.shape[1]):
        o_ref[:, s:s + n] = jnp.dot(h, w_ref[:, s:s + n], preferred_element_type=F32)


def _mod_spec(rows, layer, col, tiles_per_seq):
    return pl.BlockSpec((None, None, rows, D_MODEL), lambda i: (layer, i // tiles_per_seq, 0, col))


def _resident(shape):
    return pl.BlockSpec(shape, lambda *_: tuple(0 for _ in shape), pipeline_mode=pl.Buffered(1))


def _normmod_matmul(x, nw, mod, layer, sc_col, sh_col, w_bf, tm, name):
    t = x.shape[0]
    n = w_bf.shape[1]
    rows = mod.shape[2]
    tps = (t // mod.shape[1]) // tm
    return pl.pallas_call(
        _nm_body,
        grid=(t // tm,),
        in_specs=[
            pl.BlockSpec((tm, D_MODEL), lambda i: (i, 0)),
            _resident((1, D_MODEL)),
            _mod_spec(rows, layer, sc_col, tps),
            _mod_spec(rows, layer, sh_col, tps),
            _resident((D_MODEL, n)),
        ],
        out_specs=pl.BlockSpec((tm, n), lambda i: (i, 0)),
        out_shape=jax.ShapeDtypeStruct((t, n), F32),
        compiler_params=_cp(("arbitrary",)),
        name=name,
    )(x, nw, mod, mod, w_bf)


def _ssd_prompt_body(xbc_ref, z_ref, dt_ref, cw_ref, cb_ref, dtb_ref, alog_ref, dch_ref, nw_ref, e_ref,
                     gn_ref, st_ref, cs_ref, cbuf, ht, xc_s, yd_s):
    q = SSD_CHUNK
    c = pl.program_id(1)

    @pl.when(c == 0)
    def _():
        cbuf[0:8, :] = jnp.zeros((8, SSD_CONV_DIM), F32)
        ht[...] = jnp.zeros_like(ht)

    cbuf[8:8 + q, :] = xbc_ref[...]
    for s, n in _col_chunks(SSD_CONV_DIM):
        acc = cbuf[5:5 + q, s:s + n] * cw_ref[0:1, s:s + n]
        for k in range(1, SSD_CONV):
            acc = acc + cbuf[5 + k:5 + k + q, s:s + n] * cw_ref[k:k + 1, s:s + n]
        xc_s[:, s:s + n] = _silu(acc + cb_ref[:, s:s + n])
    cs_ref[...] = cbuf[5 + q:8 + q, :]
    cbuf[0:8, :] = cbuf[q:q + 8, :]

    dt = _softplus(dt_ref[...] + dtb_ref[...])
    da = dt * (-jnp.exp(alog_ref[...]))
    ri = lax.broadcasted_iota(jnp.int32, (q, q), 0)
    ci = lax.broadcasted_iota(jnp.int32, (q, q), 1)
    causal = ri >= ci
    tri = jnp.where(causal, 1.0, 0.0).astype(BF16)
    tri_t = jnp.where(ri <= ci, 1.0, 0.0).astype(BF16)
    acs = _dot_exact_rhs(tri, da)
    acs_t = _dot_exact_lhs(da.T, tri_t)

    for g in range(SSD_GROUPS):
        cs = slice(g * 256, (g + 1) * 256)
        e_g = e_ref[:, cs]
        dte = _dot_exact_lhs(dt, e_g)
        ace = _dot_exact_lhs(acs, e_g)
        last = ace[q - 1:q, :]
        xs_g = xc_s[:, cs]
        xdt = xs_g * dte
        b_g = xc_s[:, SSD_INNER + g * SSD_STATE:SSD_INNER + (g + 1) * SSD_STATE]
        c_g = xc_s[:, SSD_INNER + SSD_GN + g * SSD_STATE:SSD_INNER + SSD_GN + (g + 1) * SSD_STATE]
        c_b = c_g.astype(BF16)
        cb = _dot_nt(c_b, b_g.astype(BF16))
        h_prev = ht[g]
        y_off = jnp.dot(c_b, h_prev.astype(BF16), preferred_element_type=F32) * jnp.exp(ace)
        xd = (jnp.exp(last - ace) * xdt).astype(BF16)
        st_t = jnp.dot(b_g.T.astype(BF16), xd, preferred_element_type=F32)
        ht[g] = jnp.exp(last) * h_prev + st_t
        xdt_b = xdt.astype(BF16)
        for r in range(SSD_HPG):
            h = g * SSD_HPG + r
            diff = acs[:, h:h + 1] - acs_t[h:h + 1, :]
            m = (cb * jnp.where(causal, jnp.exp(diff), 0.0)).astype(BF16)
            yd_s[:, r * 64:(r + 1) * 64] = jnp.dot(m, xdt_b[:, r * 64:(r + 1) * 64],
                                                   preferred_element_type=F32)
        y = yd_s[...] + y_off + xs_g * dch_ref[:, cs]
        gg = y * _silu(z_ref[:, cs])
        ms = jnp.mean(gg * gg, axis=-1, keepdims=True)
        gn_ref[:, cs] = (gg * lax.rsqrt(ms + EPS) * nw_ref[:, cs]).astype(BF16)

    @pl.when(c == pl.num_programs(1) - 1)
    def _():
        for g in range(SSD_GROUPS):
            t = ht[g].T
            for r in range(SSD_HPG):
                st_ref[g * SSD_HPG + r] = t[r * 64:(r + 1) * 64, :]


def _ssd_prompt(zx, conv_w, conv_b, dtb, alog, dch, nw, emat):
    nc = SEQ // SSD_CHUNK
    q = SSD_CHUNK
    row = lambda b, c: b * nc + c
    return pl.pallas_call(
        _ssd_prompt_body,
        grid=(BATCH, nc),
        in_specs=[
            pl.BlockSpec((q, SSD_CONV_DIM), lambda b, c: (row(b, c), 0)),
            pl.BlockSpec((q, SSD_INNER), lambda b, c: (row(b, c), 2)),
            pl.BlockSpec((q, HEADS_PAD), lambda b, c: (row(b, c), (SSD_CONV_DIM + SSD_INNER) // HEADS_PAD)),
            _resident((SSD_CONV, SSD_CONV_DIM)),
            _resident((1, SSD_CONV_DIM)),
            _resident((1, HEADS_PAD)),
            _resident((1, HEADS_PAD)),
            _resident((1, SSD_INNER)),
            _resident((1, SSD_INNER)),
            _resident((HEADS_PAD, SSD_INNER)),
        ],
        out_specs=[
            pl.BlockSpec((q, SSD_INNER), lambda b, c: (row(b, c), 0)),
            pl.BlockSpec((None, SSD_HEADS, SSD_HEAD_DIM, SSD_STATE), lambda b, c: (b, 0, 0, 0)),
            pl.BlockSpec((None, SSD_CONV - 1, SSD_CONV_DIM), lambda b, c: (b, 0, 0)),
        ],
        out_shape=[
            jax.ShapeDtypeStruct((T_PROMPT, SSD_INNER), BF16),
            jax.ShapeDtypeStruct((BATCH, SSD_HEADS, SSD_HEAD_DIM, SSD_STATE), F32),
            jax.ShapeDtypeStruct((BATCH, SSD_CONV - 1, SSD_CONV_DIM), F32),
        ],
        scratch_shapes=[
            pltpu.VMEM((8 + q, SSD_CONV_DIM), F32),
            pltpu.VMEM((SSD_GROUPS, SSD_STATE, 256), F32),
            pltpu.VMEM((q, SSD_CONV_DIM), F32),
            pltpu.VMEM((q, 256), F32),
        ],
        compiler_params=_cp(("arbitrary", "arbitrary")),
        name="ssd_prompt",
    )(zx, zx, zx, conv_w, conv_b, dtb, alog, dch, nw, emat)


def _ssd_sample_prep_body(zx_ref, cst_ref, cw_ref, cb_ref, dtb_ref, alog_ref, e_ref,
                          xc_ref, xdt_ref, dec_ref, ncs_ref):
    kc = SSD_CONV_DIM
    for s, n in _col_chunks(kc):
        xnew = zx_ref[:, s:s + n]
        acc = xnew * cw_ref[SSD_CONV - 1:SSD_CONV, s:s + n]
        for k in range(SSD_CONV - 1):
            acc = acc + cst_ref[:, k * kc + s:k * kc + s + n] * cw_ref[k:k + 1, s:s + n]
        xc_ref[:, s:s + n] = _silu(acc + cb_ref[:, s:s + n])
        for k in range(SSD_CONV - 2):
            ncs_ref[:, k * kc + s:k * kc + s + n] = cst_ref[:, (k + 1) * kc + s:(k + 1) * kc + s + n]
        ncs_ref[:, (SSD_CONV - 2) * kc + s:(SSD_CONV - 2) * kc + s + n] = xnew
    dt = _softplus(zx_ref[:, SSD_CONV_DIM + SSD_INNER:] + dtb_ref[...])
    dec = jnp.exp(dt * (-jnp.exp(alog_ref[...])))
    for s, n in _col_chunks(SSD_INNER):
        e_c = e_ref[:, s:s + n]
        xdt_ref[:, s:s + n] = xc_ref[:, s:s + n] * _dot_exact_lhs(dt, e_c)
        dec_ref[:, s:s + n] = _dot_exact_lhs(dec, e_c)


def _ssd_sample_prep(zx, cst, conv_w, conv_b, dtb, alog, emat):
    n = DEC_BATCH
    full = lambda shape: pl.BlockSpec(shape, lambda i: tuple(0 for _ in shape))
    return pl.pallas_call(
        _ssd_sample_prep_body,
        grid=(1,),
        in_specs=[full((n, ZX_COLS)), full((n, (SSD_CONV - 1) * SSD_CONV_DIM)), full((SSD_CONV, SSD_CONV_DIM)),
                  full((1, SSD_CONV_DIM)), full((1, HEADS_PAD)), full((1, HEADS_PAD)),
                  full((HEADS_PAD, SSD_INNER))],
        out_specs=[full((n, SSD_CONV_DIM)), full((n, SSD_INNER)), full((n, SSD_INNER)),
                   full((n, (SSD_CONV - 1) * SSD_CONV_DIM))],
        out_shape=[
            jax.ShapeDtypeStruct((n, SSD_CONV_DIM), F32),
            jax.ShapeDtypeStruct((n, SSD_INNER), F32),
            jax.ShapeDtypeStruct((n, SSD_INNER), F32),
            jax.ShapeDtypeStruct((n, (SSD_CONV - 1) * SSD_CONV_DIM), F32),
        ],
        compiler_params=_cp(("arbitrary",)),
        name="ssd_sample_prep",
    )(zx, cst, conv_w, conv_b, dtb, alog, emat)


SAMPLE_BS = 8


def _ssd_sample_state_body(st_ref, xt_ref, dct_ref, b_ref, c_ref, xs_ref, z_ref, dch_ref, nw_ref,
                           so_ref, gn_ref, y_s):
    for i in range(SAMPLE_BS):
        def group(g, carry, i=i):
            r0 = pl.multiple_of(g * 256, 256)
            h = st_ref[i, pl.ds(g * SSD_HPG, SSD_HPG)].reshape(256, SSD_STATE)
            xcol = xt_ref[pl.ds(r0, 256), i:i + 1]
            dcol = dct_ref[pl.ds(r0, 256), i:i + 1]
            brow = b_ref[i, pl.ds(g, 1), :]
            hn = h * dcol + xcol * brow
            so_ref[i, pl.ds(g * SSD_HPG, SSD_HPG)] = hn.reshape(SSD_HPG, SSD_HEAD_DIM, SSD_STATE)
            yall = _dot_nt(c_ref[g].astype(BF16), hn.astype(BF16))
            y_s[g, i:i + 1, :] = yall[i:i + 1, :]
            return carry
        lax.fori_loop(0, SSD_GROUPS, group, 0)
    for g in range(SSD_GROUPS):
        cs = slice(g * 256, (g + 1) * 256)
        y = y_s[g] + xs_ref[:, cs] * dch_ref[:, cs]
        gg = y * _silu(z_ref[:, cs])
        ms = jnp.mean(gg * gg, axis=-1, keepdims=True)
        gn_ref[:, cs] = (gg * lax.rsqrt(ms + EPS) * nw_ref[:, cs]).astype(BF16)


def _ssd_sample_state(state, xt, dct, b3, c3, xc, zx, dch, nw):
    bs = SAMPLE_BS
    steps = DEC_BATCH // bs
    return pl.pallas_call(
        _ssd_sample_state_body,
        grid=(steps,),
        in_specs=[
            pl.BlockSpec((bs, SSD_HEADS, SSD_HEAD_DIM, SSD_STATE), lambda s: (s, 0, 0, 0)),
            pl.BlockSpec((None, SSD_INNER, bs), lambda s: (s, 0, 0)),
            pl.BlockSpec((None, SSD_INNER, bs), lambda s: (s, 0, 0)),
            pl.BlockSpec((bs, SSD_GROUPS, SSD_STATE), lambda s: (s, 0, 0)),
            pl.BlockSpec((SSD_GROUPS, bs, SSD_STATE), lambda s: (0, s, 0)),
            pl.BlockSpec((bs, SSD_INNER), lambda s: (s, 0)),
            pl.BlockSpec((bs, SSD_INNER), lambda s: (s, 2)),
            _resident((1, SSD_INNER)),
            _resident((1, SSD_INNER)),
        ],
        out_specs=[
            pl.BlockSpec((bs, SSD_HEADS, SSD_HEAD_DIM, SSD_STATE), lambda s: (s, 0, 0, 0)),
            pl.BlockSpec((bs, SSD_INNER), lambda s: (s, 0)),
        ],
        out_shape=[
            jax.ShapeDtypeStruct(state.shape, F32),
            jax.ShapeDtypeStruct((DEC_BATCH, SSD_INNER), BF16),
        ],
        scratch_shapes=[pltpu.VMEM((SSD_GROUPS, bs, 256), F32)],
        compiler_params=_cp(("arbitrary",), 56 * 1024 * 1024),
        name="ssd_sample_state",
    )(state, xt, dct, b3, c3, xc, zx, dch, nw)


def _sconv_prompt_body(bg_ref, cg_ref, xh_ref, cw_ref, v_ref, st_ref, cbuf):
    tm = bg_ref.shape[0]
    j = pl.program_id(1)

    @pl.when(j == 0)
    def _():
        cbuf[0:8, :] = jnp.zeros((8, D_MODEL), F32)

    cbuf[8:8 + tm, :] = cg_ref[...] * xh_ref[...]
    acc = cbuf[6:6 + tm, :] * cw_ref[0:1, :]
    for k in range(1, SC_WIDTH):
        acc = acc + cbuf[6 + k:6 + k + tm, :] * cw_ref[k:k + 1, :]
    v_ref[...] = (bg_ref[...] * acc).astype(BF16)
    st_ref[...] = cbuf[6 + tm:8 + tm, :]
    cbuf[0:8, :] = cbuf[tm:tm + 8, :]


def _sconv_prompt(bcx, conv_w, tm=256):
    nt = SEQ // tm
    row = lambda b, j: b * nt + j
    return pl.pallas_call(
        _sconv_prompt_body,
        grid=(BATCH, nt),
        in_specs=[
            pl.BlockSpec((tm, D_MODEL), lambda b, j: (row(b, j), 0)),
            pl.BlockSpec((tm, D_MODEL), lambda b, j: (row(b, j), 1)),
            pl.BlockSpec((tm, D_MODEL), lambda b, j: (row(b, j), 2)),
            _resident((SC_WIDTH, D_MODEL)),
        ],
        out_specs=[
            pl.BlockSpec((tm, D_MODEL), lambda b, j: (row(b, j), 0)),
            pl.BlockSpec((None, SC_WIDTH - 1, D_MODEL), lambda b, j: (b, 0, 0)),
        ],
        out_shape=[
            jax.ShapeDtypeStruct((T_PROMPT, D_MODEL), BF16),
            jax.ShapeDtypeStruct((BATCH, SC_WIDTH - 1, D_MODEL), F32),
        ],
        scratch_shapes=[pltpu.VMEM((8 + tm, D_MODEL), F32)],
        compiler_params=_cp(("arbitrary", "arbitrary")),
        name="sconv_prompt",
    )(bcx, bcx, bcx, conv_w)


def _sconv_sample_body(bcx_ref, st_ref, cw_ref, v_ref, ns_ref):
    d = D_MODEL
    u = bcx_ref[:, d:2 * d] * bcx_ref[:, 2 * d:3 * d]
    acc = u * cw_ref[SC_WIDTH - 1:SC_WIDTH, :]
    for k in range(SC_WIDTH - 1):
        acc = acc + st_ref[:, k * d:(k + 1) * d] * cw_ref[k:k + 1, :]
    v_ref[...] = (bcx_ref[:, 0:d] * acc).astype(BF16)
    for k in range(SC_WIDTH - 2):
        ns_ref[:, k * d:(k + 1) * d] = st_ref[:, (k + 1) * d:(k + 2) * d]
    ns_ref[:, (SC_WIDTH - 2) * d:] = u


def _sconv_sample(bcx, st, conv_w):
    n = DEC_BATCH
    full = lambda shape: pl.BlockSpec(shape, lambda i: tuple(0 for _ in shape))
    return pl.pallas_call(
        _sconv_sample_body,
        grid=(1,),
        in_specs=[full((n, 3 * D_MODEL)), full((n, (SC_WIDTH - 1) * D_MODEL)), full((SC_WIDTH, D_MODEL))],
        out_specs=[full((n, D_MODEL)), full((n, (SC_WIDTH - 1) * D_MODEL))],
        out_shape=[jax.ShapeDtypeStruct((n, D_MODEL), BF16),
                   jax.ShapeDtypeStruct((n, (SC_WIDTH - 1) * D_MODEL), F32)],
        compiler_params=_cp(("arbitrary",)),
        name="sconv_sample",
    )(bcx, st, conv_w)


def _outproj_body(x_ref, a_ref, w_ref, g1_ref, nw_ref, sc_ref, sh_ref, rwt_ref, rb_ref,
                  x1_ref, h2_ref, idx_ref, pr_ref):
    m = jnp.dot(a_ref[...], w_ref[...], preferred_element_type=F32)
    x1 = x_ref[...] + g1_ref[...] * m
    x1_ref[...] = x1
    h2 = _normmod(x1, nw_ref[...], sc_ref[...], sh_ref[...])
    h2_ref[...] = h2
    h_hi = h2.astype(BF16)
    h_lo = (h2 - h_hi.astype(F32)).astype(BF16)
    rw = rwt_ref[...]
    w_hi = rw.astype(BF16)
    w_lo = (rw - w_hi.astype(F32)).astype(BF16)
    logits = _dot_nt(w_hi, h_hi) + _dot_nt(w_hi, h_lo) + _dot_nt(w_lo, h_hi) + rb_ref[...]
    eio = lax.broadcasted_iota(jnp.int32, logits.shape, 0)
    vals, idxs = [], []
    for _ in range(TOP_K):
        mx = jnp.max(logits, axis=0, keepdims=True)
        sel = jnp.min(jnp.where(logits == mx, eio, N_EXPERTS), axis=0, keepdims=True)
        vals.append(mx)
        idxs.append(sel)
        logits = jnp.where(eio == sel, -jnp.inf, logits)
    ex = [jnp.exp(v - vals[0]) for v in vals]
    tot = ex[0] + ex[1] + ex[2] + ex[3]
    idx_ref[...] = jnp.concatenate(idxs, axis=0)
    pr_ref[...] = jnp.concatenate([e / tot for e in ex], axis=0)


def _outproj(x, a, w_bf, mod, layer, nw, rwt, rb, tm, name):
    t = x.shape[0]
    k = a.shape[1]
    rows = mod.shape[2]
    tps = (t // mod.shape[1]) // tm
    return pl.pallas_call(
        _outproj_body,
        grid=(t // tm,),
        in_specs=[
            pl.BlockSpec((tm, D_MODEL), lambda i: (i, 0)),
            pl.BlockSpec((tm, k), lambda i: (i, 0)),
            _resident((k, D_MODEL)),
            _mod_spec(rows, layer, 2, tps),
            _resident((1, D_MODEL)),
            _mod_spec(rows, layer, 4, tps),
            _mod_spec(rows, layer, 3, tps),
            _resident((N_EXPERTS, D_MODEL)),
            _resident((N_EXPERTS, 1)),
        ],
        out_specs=[
            pl.BlockSpec((tm, D_MODEL), lambda i: (i, 0)),
            pl.BlockSpec((tm, D_MODEL), lambda i: (i, 0)),
            pl.BlockSpec((TOP_K, tm), lambda i: (0, i)),
            pl.BlockSpec((TOP_K, tm), lambda i: (0, i)),
        ],
        out_shape=[
            jax.ShapeDtypeStruct((t, D_MODEL), F32),
            jax.ShapeDtypeStruct((t, D_MODEL), F32),
            jax.ShapeDtypeStruct((TOP_K, t), jnp.int32),
            jax.ShapeDtypeStruct((TOP_K, t), F32),
        ],
        compiler_params=_cp(("arbitrary",)),
        name=name,
    )(x, a, w_bf, mod, nw, mod, mod, rwt, rb)


def _moe_body(te_ref, na_ref, tok_ref, tokn_ref, dst_ref, h_hbm, wgu_ref, bgu_ref, wd_ref, bd_ref, pr_ref,
              perm_ref, y_hbm, xbuf, ybuf, wgu_s, wd_s, act_s, gsem, ssem):
    i = pl.program_id(0)
    n_active = na_ref[0]
    slot = i % 2

    def gather(rows_ref, sl):
        def issue(r, carry):
            pltpu.make_async_copy(h_hbm.at[pl.ds(rows_ref[0, r], 1)], xbuf.at[sl, pl.ds(r, 1)],
                                  gsem.at[sl]).start()
            return carry
        lax.fori_loop(0, MOE_TM, issue, 0, unroll=8)

    @pl.when(i == 0)
    def _():
        gather(tok_ref, 0)
        ybuf[...] = jnp.zeros_like(ybuf)
        pad = T_PAD - T_ALL
        fills = [pltpu.make_async_copy(ybuf.at[pl.ds(0, pad)], y_hbm.at[pl.ds(k * T_PAD + T_ALL, pad)], ssem.at[0])
                 for k in range(TOP_K)]
        fills.append(pltpu.make_async_copy(ybuf, y_hbm.at[pl.ds(TOP_K * T_PAD, MOE_TM)], ssem.at[0]))
        for f in fills:
            f.start()
        for f in fills:
            f.wait()

    @pl.when(i + 1 < n_active)
    def _():
        gather(tokn_ref, 1 - slot)

    @pl.when(i < n_active)
    def _():
        e_prev = te_ref[jnp.maximum(i - 1, 0)]

        @pl.when(jnp.logical_or(i == 0, e_prev != te_ref[i]))
        def _():
            for b in range(8):
                cs = slice(b * 256, (b + 1) * 256)
                wgu_s[:, cs] = jnp.dot(wgu_ref[:, cs].astype(BF16), perm_ref[...],
                                       preferred_element_type=F32).astype(BF16)
            wd_s[...] = wd_ref[...].astype(BF16)

        pltpu.make_async_copy(xbuf.at[slot], xbuf.at[slot], gsem.at[slot]).wait()
        x = xbuf[slot].astype(BF16)
        for b in range(8):
            cs = slice(b * 256, (b + 1) * 256)
            gu = jnp.dot(x, wgu_s[:, cs], preferred_element_type=F32) + bgu_ref[:, cs]
            gate = jnp.minimum(gu[:, :128], SWIGLU_LIMIT)
            up = jnp.clip(gu[:, 128:], -SWIGLU_LIMIT, SWIGLU_LIMIT)
            act_s[:, b * 128:(b + 1) * 128] = ((up + 1.0) * (gate * _sigmoid(SWIGLU_ALPHA * gate))).astype(BF16)
        y = jnp.dot(act_s[...], wd_s[...], preferred_element_type=F32) + bd_ref[...]

        @pl.when(i >= 1)
        def _():
            pltpu.make_async_copy(ybuf, ybuf, ssem.at[0]).wait()

        ybuf[...] = y * pr_ref[...]

        def scatter(r, carry):
            pltpu.make_async_copy(ybuf.at[pl.ds(r, 1)], y_hbm.at[pl.ds(dst_ref[0, r], 1)], ssem.at[0]).start()
            return carry
        lax.fori_loop(0, MOE_TM, scatter, 0, unroll=8)

        @pl.when(i == n_active - 1)
        def _():
            pltpu.make_async_copy(ybuf, ybuf, ssem.at[0]).wait()


def _moe_call(layer, h_all, tile_expert, n_active, row_tok, row_dst, row_prob, wgu, bgu_perm, wd, bd, perm):
    smem_rows = lambda f: pl.BlockSpec((None, 1, MOE_TM), f, memory_space=pltpu.SMEM)
    grid_spec = pltpu.PrefetchScalarGridSpec(
        num_scalar_prefetch=2,
        grid=(MOE_TILES,),
        in_specs=[
            smem_rows(lambda i, te, na: (i, 0, 0)),
            smem_rows(lambda i, te, na: (jnp.minimum(i + 1, MOE_TILES - 1), 0, 0)),
            smem_rows(lambda i, te, na: (i, 0, 0)),
            pl.BlockSpec(memory_space=pl.ANY),
            pl.BlockSpec((None, None, D_MODEL, 2 * D_MODEL), lambda i, te, na: (layer, te[i], 0, 0)),
            pl.BlockSpec((None, None, 1, 2 * D_MODEL), lambda i, te, na: (layer, te[i], 0, 0)),
            pl.BlockSpec((None, None, D_MODEL, D_MODEL), lambda i, te, na: (layer, te[i], 0, 0)),
            pl.BlockSpec((None, None, 1, D_MODEL), lambda i, te, na: (layer, te[i], 0, 0)),
            pl.BlockSpec((MOE_TM, 1), lambda i, te, na: (i, 0)),
            pl.BlockSpec((256, 256), lambda i, te, na: (0, 0)),
        ],
        out_specs=pl.BlockSpec(memory_space=pl.ANY),
        scratch_shapes=[
            pltpu.VMEM((2, MOE_TM, D_MODEL), F32),
            pltpu.VMEM((MOE_TM, D_MODEL), F32),
            pltpu.VMEM((D_MODEL, 2 * D_MODEL), BF16),
            pltpu.VMEM((D_MODEL, D_MODEL), BF16),
            pltpu.VMEM((MOE_TM, D_MODEL), BF16),
            pltpu.SemaphoreType.DMA((2,)),
            pltpu.SemaphoreType.DMA((1,)),
        ],
    )
    return pl.pallas_call(
        _moe_body,
        grid_spec=grid_spec,
        out_shape=jax.ShapeDtypeStruct((Y_ROWS, D_MODEL), F32),
        compiler_params=_cp(("arbitrary",), 56 * 1024 * 1024),
        name="moe_experts",
    )(tile_expert, n_active, row_tok, row_tok, row_dst, h_all, wgu, bgu_perm, wd, bd, row_prob, perm)


def _route(idx_t, pr_t):
    e_flat = idx_t.reshape(-1)
    p_flat = pr_t.reshape(-1)
    order = jnp.argsort(e_flat, stable=True).astype(jnp.int32)
    counts = jnp.sum(e_flat[:, None] == jnp.arange(N_EXPERTS, dtype=jnp.int32)[None, :], axis=0,
                     dtype=jnp.int32)
    tiles = (counts + MOE_TM - 1) // MOE_TM
    tile_end = jnp.cumsum(tiles)
    tile_start = tile_end - tiles
    cstart = jnp.cumsum(counts) - counts
    n_active = tile_end[-1]
    ti = jnp.arange(MOE_TILES, dtype=jnp.int32)
    te = jnp.minimum(jnp.searchsorted(tile_end, ti, side="right").astype(jnp.int32), N_EXPERTS - 1)
    te = jnp.where(ti < n_active, te, te[jnp.maximum(n_active - 1, 0)])
    p = jnp.arange(P_PAD, dtype=jnp.int32)
    pe = te[p // MOE_TM]
    r = p - tile_start[pe] * MOE_TM
    valid = jnp.logical_and(r < counts[pe], (p // MOE_TM) < n_active)
    src = order[jnp.clip(cstart[pe] + r, 0, N_ASSIGN - 1)]
    k = src // T_ALL
    t = src - k * T_ALL
    row_tok = jnp.where(valid, t, 0).reshape(MOE_TILES, 1, MOE_TM)
    row_dst = jnp.where(valid, k * T_PAD + t, TOP_K * T_PAD + (p % MOE_TM)).reshape(MOE_TILES, 1, MOE_TM)
    row_prob = jnp.where(valid, p_flat[src], 0.0).reshape(P_PAD, 1)
    return te, n_active.reshape(1).astype(jnp.int32), row_tok, row_dst, row_prob


def _combine_body(x_ref, g_ref, y0, y1, y2, y3, o_ref):
    o_ref[...] = x_ref[...] + g_ref[...] * (y0[...] + y1[...] + y2[...] + y3[...])


def _combine_norm_body(x_ref, g_ref, y0, y1, y2, y3, nw_ref, o_ref):
    x = x_ref[...] + g_ref[...] * (y0[...] + y1[...] + y2[...] + y3[...])
    o_ref[...] = x * lax.rsqrt(jnp.mean(x * x, axis=-1, keepdims=True) + EPS) * nw_ref[...]


def _combine(x, mod, layer, y_all, tok0, tm, final_w=None, name="combine"):
    t = x.shape[0]
    rows = mod.shape[2]
    tps = (t // mod.shape[1]) // tm
    yspec = lambda k: pl.BlockSpec((tm, D_MODEL), lambda i: ((k * T_PAD + tok0) // tm + i, 0))
    in_specs = [pl.BlockSpec((tm, D_MODEL), lambda i: (i, 0)), _mod_spec(rows, layer, 5, tps),
                yspec(0), yspec(1), yspec(2), yspec(3)]
    args = [x, mod, y_all, y_all, y_all, y_all]
    body = _combine_body
    if final_w is not None:
        in_specs.append(_resident((1, D_MODEL)))
        args.append(final_w)
        body = _combine_norm_body
    return pl.pallas_call(
        body,
        grid=(t // tm,),
        in_specs=in_specs,
        out_specs=pl.BlockSpec((tm, D_MODEL), lambda i: (i, 0)),
        out_shape=jax.ShapeDtypeStruct((t, D_MODEL), F32),
        compiler_params=_cp(("arbitrary",)),
        name=name,
    )(*args)


def _moe_layer(layer, h2_p, h2_s, idx_p, idx_s, pr_p, pr_s, moe_w_gate_up, bgu_perm, moe_w_down, moe_b_down, perm):
    h_all = jnp.concatenate([h2_p, h2_s, jnp.zeros((T_PAD - T_ALL, D_MODEL), F32)], axis=0)
    idx_t = jnp.concatenate([idx_p, idx_s], axis=1)
    pr_t = jnp.concatenate([pr_p, pr_s], axis=1)
    te, n_active, row_tok, row_dst, row_prob = _route(idx_t, pr_t)
    return _moe_call(layer, h_all, te, n_active, row_tok, row_dst, row_prob, moe_w_gate_up, bgu_perm,
                     moe_w_down, moe_b_down.reshape(-1, N_EXPERTS, 1, D_MODEL), perm)


def kernel(x_prompt, x_sample, c_prompt, c_sample, state_ssm, state_ssm_conv, state_sconv, ada_w, ada_b, norm1_w, norm2_w, ssd_w_in, ssd_conv_w, ssd_conv_b, ssd_dt_bias, ssd_A_log, ssd_D, ssd_norm_w, ssd_w_out, sc_w_in, sc_conv_w, sc_w_out, router_w, router_b, moe_w_gate_up, moe_b_gate_up, moe_w_down, moe_b_down, final_norm_w):
    tm_p = 256
    tm_s = DEC_BATCH
    depth = ada_w.shape[0]

    w_in0 = jnp.concatenate(
        [ssd_w_in[:, SSD_INNER:SSD_INNER + SSD_CONV_DIM], ssd_w_in[:, :SSD_INNER],
         ssd_w_in[:, SSD_INNER + SSD_CONV_DIM:], jnp.zeros((D_MODEL, HEADS_PAD - SSD_HEADS), F32)],
        axis=1).astype(BF16)
    w_out0 = ssd_w_out.astype(BF16)
    w_in1 = sc_w_in.astype(BF16)
    w_out1 = sc_w_out.astype(BF16)
    pad_h = lambda v: jnp.concatenate([v, jnp.zeros((HEADS_PAD - SSD_HEADS,), F32)]).reshape(1, HEADS_PAD)
    dtb = pad_h(ssd_dt_bias)
    alog = pad_h(ssd_A_log)
    dch = jnp.repeat(ssd_D, SSD_HEAD_DIM).reshape(1, SSD_INNER)
    ssd_nw = ssd_norm_w.reshape(1, SSD_INNER)
    conv_b = ssd_conv_b.reshape(1, SSD_CONV_DIM)
    emat = (jnp.arange(SSD_INNER, dtype=jnp.int32)[None, :] // SSD_HEAD_DIM
            == jnp.arange(HEADS_PAD, dtype=jnp.int32)[:, None]).astype(BF16)
    jj = jnp.arange(256, dtype=jnp.int32)
    perm = (jj[None, :] == jnp.where(jj % 2 == 0, jj // 2, 128 + jj // 2)[:, None]).astype(BF16)
    bgu_perm = moe_b_gate_up.reshape(depth, N_EXPERTS, 8, 128, 2).transpose(0, 1, 2, 4, 3).reshape(
        depth, N_EXPERTS, 1, 2 * D_MODEL)
    rwt = jnp.transpose(router_w, (0, 2, 1))
    rb = router_b.reshape(depth, N_EXPERTS, 1)
    n1 = norm1_w.reshape(depth, 1, D_MODEL)
    n2 = norm2_w.reshape(depth, 1, D_MODEL)

    mod = _ada_call(jnp.concatenate([c_prompt, c_sample], axis=0), ada_w, ada_b)
    mod_p = mod[:, :BATCH].reshape(depth, BATCH, 1, 6 * D_MODEL)
    mod_s = mod[:, BATCH:].reshape(depth, 1, DEC_BATCH, 6 * D_MODEL)

    xp = x_prompt.reshape(T_PROMPT, D_MODEL)
    xs = x_sample.reshape(DEC_BATCH, D_MODEL)

    zx_p = _normmod_matmul(xp, n1[0], mod_p, 0, 1, 0, w_in0, tm_p, "inproj0_prompt")
    gn_p, ssm_p, ssm_conv_p = _ssd_prompt(zx_p, ssd_conv_w, conv_b, dtb, alog, dch, ssd_nw, emat)
    zx_s = _normmod_matmul(xs, n1[0], mod_s, 0, 1, 0, w_in0, tm_s, "inproj0_sample")
    xc_s, xdt_s, dec_s, ncs_s = _ssd_sample_prep(
        zx_s, state_ssm_conv.reshape(DEC_BATCH, (SSD_CONV - 1) * SSD_CONV_DIM), ssd_conv_w, conv_b, dtb, alog, emat)
    steps = DEC_BATCH // SAMPLE_BS
    to_cols = lambda a: a.reshape(steps, SAMPLE_BS, SSD_INNER).transpose(0, 2, 1)
    b3 = xc_s[:, SSD_INNER:SSD_INNER + SSD_GN].reshape(DEC_BATCH, SSD_GROUPS, SSD_STATE)
    c3 = xc_s[:, SSD_INNER + SSD_GN:].reshape(DEC_BATCH, SSD_GROUPS, SSD_STATE).transpose(1, 0, 2)
    ssm_s, gn_s = _ssd_sample_state(state_ssm, to_cols(xdt_s), to_cols(dec_s), b3, c3, xc_s, zx_s, dch, ssd_nw)
    ssm_conv_s = ncs_s.reshape(DEC_BATCH, SSD_CONV - 1, SSD_CONV_DIM)

    x1_p, h2_p, idx_p, pr_p = _outproj(xp, gn_p, w_out0, mod_p, 0, n2[0], rwt[0], rb[0], tm_p, "outproj0_prompt")
    x1_s, h2_s, idx_s, pr_s = _outproj(xs, gn_s, w_out0, mod_s, 0, n2[0], rwt[0], rb[0], tm_s, "outproj0_sample")
    y0 = _moe_layer(0, h2_p, h2_s, idx_p, idx_s, pr_p, pr_s, moe_w_gate_up, bgu_perm, moe_w_down, moe_b_down, perm)
    x2_p = _combine(x1_p, mod_p, 0, y0, 0, tm_p, name="combine0_prompt")
    x2_s = _combine(x1_s, mod_s, 0, y0, T_PROMPT, tm_s, name="combine0_sample")

    bcx_p = _normmod_matmul(x2_p, n1[1], mod_p, 1, 1, 0, w_in1, tm_p, "inproj1_prompt")
    v_p, sconv_p = _sconv_prompt(bcx_p, sc_conv_w)
    bcx_s = _normmod_matmul(x2_s, n1[1], mod_s, 1, 1, 0, w_in1, tm_s, "inproj1_sample")
    v_s, nsc_s = _sconv_sample(bcx_s, state_sconv.reshape(DEC_BATCH, (SC_WIDTH - 1) * D_MODEL), sc_conv_w)
    sconv_s = nsc_s.reshape(DEC_BATCH, SC_WIDTH - 1, D_MODEL)

    x3_p, h4_p, idx_p, pr_p = _outproj(x2_p, v_p, w_out1, mod_p, 1, n2[1], rwt[1], rb[1], tm_p, "outproj1_prompt")
    x3_s, h4_s, idx_s, pr_s = _outproj(x2_s, v_s, w_out1, mod_s, 1, n2[1], rwt[1], rb[1], tm_s, "outproj1_sample")
    y1 = _moe_layer(1, h4_p, h4_s, idx_p, idx_s, pr_p, pr_s, moe_w_gate_up, bgu_perm, moe_w_down, moe_b_down, perm)
    fw = final_norm_w.reshape(1, D_MODEL)
    y_p = _combine(x3_p, mod_p, 1, y1, 0, tm_p, final_w=fw, name="final_prompt")
    y_s = _combine(x3_s, mod_s, 1, y1, T_PROMPT, tm_s, final_w=fw, name="final_sample")

    return (y_p.reshape(BATCH, SEQ, D_MODEL), y_s.reshape(DEC_BATCH, 1, D_MODEL), ssm_p, ssm_conv_p, sconv_p,
            ssm_s, ssm_conv_s, sconv_s)
```

```python
import functools

import jax
import jax.numpy as jnp
from jax import lax
from jax.experimental import pallas as pl
from jax.experimental.pallas import tpu as pltpu

F32 = jnp.float32
BF16 = jnp.bfloat16

D_MODEL = 1024
BATCH = 8
SEQ = 2048
DEC_BATCH = 128
SSD_INNER = 2048
SSD_HEAD_DIM = 64
SSD_HEADS = 32
SSD_GROUPS = 8
SSD_HPG = 4
SSD_STATE = 128
SSD_CONV = 4
SSD_CHUNK = 128
SSD_GN = 1024
SSD_CONV_DIM = 4096
SC_WIDTH = 3
N_EXPERTS = 32
TOP_K = 4
SWIGLU_LIMIT = 7.0
SWIGLU_ALPHA = 1.702
EPS = 1e-5

LANES = 128
HEADS_PAD = LANES
ZX_COLS = SSD_CONV_DIM + SSD_INNER + HEADS_PAD
T_PROMPT = BATCH * SEQ
T_ALL = T_PROMPT + DEC_BATCH
MOE_TM = 256
T_PAD = 16640
N_ASSIGN = TOP_K * T_ALL
MOE_TILES = (N_ASSIGN + N_EXPERTS * (MOE_TM - 1)) // MOE_TM + 1
P_PAD = MOE_TILES * MOE_TM
Y_ROWS = TOP_K * T_PAD + 2 * MOE_TM
VMEM_LIMIT = 48 * 1024 * 1024


def _cp(sem, vmem=VMEM_LIMIT):
    return pltpu.CompilerParams(dimension_semantics=sem, vmem_limit_bytes=vmem)


def _sigmoid(x):
    return 1.0 / (1.0 + jnp.exp(-x))


def _silu(x):
    return x * _sigmoid(x)


def _softplus(x):
    return jnp.maximum(x, 0.0) + jnp.log1p(jnp.exp(-jnp.abs(x)))


def _normmod(x, nw, sc, sh):
    y = x * lax.rsqrt(jnp.mean(x * x, axis=-1, keepdims=True) + EPS)
    return y * nw * (1.0 + sc) + sh


def _split3(x):
    hi = x.astype(BF16)
    r1 = x - hi.astype(F32)
    mid = r1.astype(BF16)
    lo = (r1 - mid.astype(F32)).astype(BF16)
    return hi, mid, lo


def _dot_exact_lhs(x, m01):
    hi, mid, lo = _split3(x)
    d = functools.partial(jnp.dot, preferred_element_type=F32)
    return d(hi, m01) + d(mid, m01) + d(lo, m01)


def _dot_exact_rhs(m01, x):
    hi, mid, lo = _split3(x)
    d = functools.partial(jnp.dot, preferred_element_type=F32)
    return d(m01, hi) + d(m01, mid) + d(m01, lo)


def _dot_nt(a, b):
    return lax.dot_general(a, b, (((1,), (1,)), ((), ())), preferred_element_type=F32)


def _split2(x):
    hi = x.astype(BF16)
    return hi, (x - hi.astype(F32)).astype(BF16)


def _dot_x3(a, b, nt=False):
    d = _dot_nt if nt else functools.partial(jnp.dot, preferred_element_type=F32)
    return d(a[0], b[0]) + d(a[1], b[0]) + d(a[0], b[1])


ROW_SUB = D_MODEL // LANES


def _load_rows(ref):
    tm = ref.shape[0] // ROW_SUB
    return jnp.concatenate([ref[pl.ds(c, tm, stride=ROW_SUB), :] for c in range(ROW_SUB)], axis=1)


def _store_rows(ref, x):
    tm = ref.shape[0] // ROW_SUB
    for c in range(ROW_SUB):
        ref[pl.ds(c, tm, stride=ROW_SUB), :] = x[:, c * LANES:(c + 1) * LANES]


def _ada_body(c_ref, w_ref, b_ref, o_ref):
    o_ref[...] = _dot_x3(_split2(_silu(c_ref[...])), _split2(w_ref[...])) + b_ref[...]


def _ada_call(c_all, ada_w, ada_b):
    n = c_all.shape[0]
    depth = ada_w.shape[0]
    return pl.pallas_call(
        _ada_body,
        grid=(depth, 6),
        in_specs=[
            pl.BlockSpec((n, D_MODEL), lambda l, j: (0, 0)),
            pl.BlockSpec((None, D_MODEL, D_MODEL), lambda l, j: (l, 0, j)),
            pl.BlockSpec((None, 1, D_MODEL), lambda l, j: (l, 0, j)),
        ],
        out_specs=pl.BlockSpec((None, n, D_MODEL), lambda l, j: (l, 0, j)),
        out_shape=jax.ShapeDtypeStruct((depth, n, 6 * D_MODEL), F32),
        compiler_params=_cp(("arbitrary", "arbitrary")),
        name="ada_mod",
    )(c_all, ada_w, ada_b.reshape(depth, 1, 6 * D_MODEL))


def _col_chunks(n, step=512):
    return [(s, min(step, n - s)) for s in range(0, n, step)]


def _rows_x3(a_hi, a_hilo, w_ref, s, n):
    tm = a_hi.shape[0]
    r = jnp.dot(a_hilo, w_ref[0, :, s:s + n], preferred_element_type=F32)
    return r[:tm] + r[tm:] + jnp.dot(a_hi, w_ref[1, :, s:s + n], preferred_element_type=F32)


def _nm_body(x_ref, nw_ref, sc_ref, sh_ref, w_ref, o_ref):
    h = _normmod(x_ref[...], nw_ref[...], sc_ref[...], sh_ref[...])
    if len(w_ref.shape) == 3:
        h_hi, h_lo = _split2(h)
        h_hilo = jnp.concatenate([h_hi, h_lo], axis=0)
        for s, n in _col_chunks(o_ref.shape[1]):
            o_ref[:, s:s + n] = _rows_x3(h_hi, h_hilo, w_ref, s, n)
    else:
        h = h.astype(BF16)
        for s, n in _col_chunks(o_ref.shape[1]):
            o_ref[:, s:s + n] = jnp.dot(h, w_ref[:, s:s + n], preferred_element_type=F32)


def _mod_spec(rows, layer, col, tiles_per_seq):
    return pl.BlockSpec((None, None, rows, D_MODEL), lambda i: (layer, i // tiles_per_seq, 0, col))


def _resident(shape):
    return pl.BlockSpec(shape, lambda *_: tuple(0 for _ in shape), pipeline_mode=pl.Buffered(1))


def _normmod_matmul(x, nw, mod, layer, sc_col, sh_col, w_bf, tm, name):
    t = x.shape[0]
    n = w_bf.shape[-1]
    rows = mod.shape[2]
    tps = (t // mod.shape[1]) // tm
    return pl.pallas_call(
        _nm_body,
        grid=(t // tm,),
        in_specs=[
            pl.BlockSpec((tm, D_MODEL), lambda i: (i, 0)),
            _resident((1, D_MODEL)),
            _mod_spec(rows, layer, sc_col, tps),
            _mod_spec(rows, layer, sh_col, tps),
            _resident(w_bf.shape),
        ],
        out_specs=pl.BlockSpec((tm, n), lambda i: (i, 0)),
        out_shape=jax.ShapeDtypeStruct((t, n), F32),
        compiler_params=_cp(("arbitrary",)),
        name=name,
    )(x, nw, mod, mod, w_bf)


def _ssd_prompt_body(xbc_ref, z_ref, dt_ref, cw_ref, cb_ref, dtb_ref, alog_ref, dch_ref, nw_ref, e_ref,
                     gn_ref, st_ref, cs_ref, cbuf, ht, xc_s, yd_s):
    q = SSD_CHUNK
    c = pl.program_id(1)

    @pl.when(c == 0)
    def _():
        cbuf[0:8, :] = jnp.zeros((8, SSD_CONV_DIM), F32)
        ht[...] = jnp.zeros_like(ht)

    cbuf[8:8 + q, :] = xbc_ref[...]
    for s, n in _col_chunks(SSD_CONV_DIM):
        acc = cbuf[5:5 + q, s:s + n] * cw_ref[0:1, s:s + n]
        for k in range(1, SSD_CONV):
            acc = acc + cbuf[5 + k:5 + k + q, s:s + n] * cw_ref[k:k + 1, s:s + n]
        xc_s[:, s:s + n] = _silu(acc + cb_ref[:, s:s + n])
    cs_ref[...] = cbuf[5 + q:8 + q, :]
    cbuf[0:8, :] = cbuf[q:q + 8, :]

    dt = _softplus(dt_ref[...] + dtb_ref[...])
    da = dt * (-jnp.exp(alog_ref[...]))
    ri = lax.broadcasted_iota(jnp.int32, (q, q), 0)
    ci = lax.broadcasted_iota(jnp.int32, (q, q), 1)
    causal = ri >= ci
    tri = jnp.where(causal, 1.0, 0.0).astype(BF16)
    tri_t = jnp.where(ri <= ci, 1.0, 0.0).astype(BF16)
    acs = _dot_exact_rhs(tri, da)
    acs_t = _dot_exact_lhs(da.T, tri_t)

    for g in range(SSD_GROUPS):
        cs = slice(g * 256, (g + 1) * 256)
        e_g = e_ref[:, cs]
        dte = _dot_exact_lhs(dt, e_g)
        ace = _dot_exact_lhs(acs, e_g)
        last = ace[q - 1:q, :]
        xs_g = xc_s[:, cs]
        xdt = xs_g * dte
        b_g = xc_s[:, SSD_INNER + g * SSD_STATE:SSD_INNER + (g + 1) * SSD_STATE]
        c_g = xc_s[:, SSD_INNER + SSD_GN + g * SSD_STATE:SSD_INNER + SSD_GN + (g + 1) * SSD_STATE]
        c_2 = _split2(c_g)
        cb = _dot_x3(c_2, _split2(b_g), nt=True)
        h_prev = ht[g]
        y_off = _dot_x3(c_2, _split2(h_prev)) * jnp.exp(ace)
        st_t = _dot_x3(_split2(b_g.T), _split2(jnp.exp(last - ace) * xdt))
        ht[g] = jnp.exp(last) * h_prev + st_t
        xdt_hi, xdt_lo = _split2(xdt)
        for r in range(SSD_HPG):
            h = g * SSD_HPG + r
            hs = slice(r * 64, (r + 1) * 64)
            diff = acs[:, h:h + 1] - acs_t[h:h + 1, :]
            m_2 = _split2(cb * jnp.where(causal, jnp.exp(diff), 0.0))
            yd_s[:, hs] = _dot_x3(m_2, (xdt_hi[:, hs], xdt_lo[:, hs]))
        y = yd_s[...] + y_off + xs_g * dch_ref[:, cs]
        gg = y * _silu(z_ref[:, cs])
        ms = jnp.mean(gg * gg, axis=-1, keepdims=True)
        gn_ref[:, cs] = gg * lax.rsqrt(ms + EPS) * nw_ref[:, cs]

    @pl.when(c == pl.num_programs(1) - 1)
    def _():
        for g in range(SSD_GROUPS):
            t = ht[g].T
            for r in range(SSD_HPG):
                st_ref[g * SSD_HPG + r] = t[r * 64:(r + 1) * 64, :]


def _ssd_prompt(zx, conv_w, conv_b, dtb, alog, dch, nw, emat):
    nc = SEQ // SSD_CHUNK
    q = SSD_CHUNK
    row = lambda b, c: b * nc + c
    return pl.pallas_call(
        _ssd_prompt_body,
        grid=(BATCH, nc),
        in_specs=[
            pl.BlockSpec((q, SSD_CONV_DIM), lambda b, c: (row(b, c), 0)),
            pl.BlockSpec((q, SSD_INNER), lambda b, c: (row(b, c), 2)),
            pl.BlockSpec((q, HEADS_PAD), lambda b, c: (row(b, c), (SSD_CONV_DIM + SSD_INNER) // HEADS_PAD)),
            _resident((SSD_CONV, SSD_CONV_DIM)),
            _resident((1, SSD_CONV_DIM)),
            _resident((1, HEADS_PAD)),
            _resident((1, HEADS_PAD)),
            _resident((1, SSD_INNER)),
            _resident((1, SSD_INNER)),
            _resident((HEADS_PAD, SSD_INNER)),
        ],
        out_specs=[
            pl.BlockSpec((q, SSD_INNER), lambda b, c: (row(b, c), 0)),
            pl.BlockSpec((None, SSD_HEADS, SSD_HEAD_DIM, SSD_STATE), lambda b, c: (b, 0, 0, 0)),
            pl.BlockSpec((None, SSD_CONV - 1, SSD_CONV_DIM), lambda b, c: (b, 0, 0)),
        ],
        out_shape=[
            jax.ShapeDtypeStruct((T_PROMPT, SSD_INNER), F32),
            jax.ShapeDtypeStruct((BATCH, SSD_HEADS, SSD_HEAD_DIM, SSD_STATE), F32),
            jax.ShapeDtypeStruct((BATCH, SSD_CONV - 1, SSD_CONV_DIM), F32),
        ],
        scratch_shapes=[
            pltpu.VMEM((8 + q, SSD_CONV_DIM), F32),
            pltpu.VMEM((SSD_GROUPS, SSD_STATE, 256), F32),
            pltpu.VMEM((q, SSD_CONV_DIM), F32),
            pltpu.VMEM((q, 256), F32),
        ],
        compiler_params=_cp(("arbitrary", "arbitrary")),
        name="ssd_prompt",
    )(zx, zx, zx, conv_w, conv_b, dtb, alog, dch, nw, emat)


def _ssd_sample_prep_body(zx_ref, cst_ref, cw_ref, cb_ref, dtb_ref, alog_ref, e_ref,
                          xc_ref, xdt_ref, dec_ref, ncs_ref):
    kc = SSD_CONV_DIM
    for s, n in _col_chunks(kc):
        xnew = zx_ref[:, s:s + n]
        acc = xnew * cw_ref[SSD_CONV - 1:SSD_CONV, s:s + n]
        for k in range(SSD_CONV - 1):
            acc = acc + cst_ref[:, k * kc + s:k * kc + s + n] * cw_ref[k:k + 1, s:s + n]
        xc_ref[:, s:s + n] = _silu(acc + cb_ref[:, s:s + n])
        for k in range(SSD_CONV - 2):
            ncs_ref[:, k * kc + s:k * kc + s + n] = cst_ref[:, (k + 1) * kc + s:(k + 1) * kc + s + n]
        ncs_ref[:, (SSD_CONV - 2) * kc + s:(SSD_CONV - 2) * kc + s + n] = xnew
    dt = _softplus(zx_ref[:, SSD_CONV_DIM + SSD_INNER:] + dtb_ref[...])
    dec = jnp.exp(dt * (-jnp.exp(alog_ref[...])))
    for s, n in _col_chunks(SSD_INNER):
        e_c = e_ref[:, s:s + n]
        xdt_ref[:, s:s + n] = xc_ref[:, s:s + n] * _dot_exact_lhs(dt, e_c)
        dec_ref[:, s:s + n] = _dot_exact_lhs(dec, e_c)


def _ssd_sample_prep(zx, cst, conv_w, conv_b, dtb, alog, emat):
    n = DEC_BATCH
    full = lambda shape: pl.BlockSpec(shape, lambda i: tuple(0 for _ in shape))
    return pl.pallas_call(
        _ssd_sample_prep_body,
        grid=(1,),
        in_specs=[full((n, ZX_COLS)), full((n, (SSD_CONV - 1) * SSD_CONV_DIM)), full((SSD_CONV, SSD_CONV_DIM)),
                  full((1, SSD_CONV_DIM)), full((1, HEADS_PAD)), full((1, HEADS_PAD)),
                  full((HEADS_PAD, SSD_INNER))],
        out_specs=[full((n, SSD_CONV_DIM)), full((n, SSD_INNER)), full((n, SSD_INNER)),
                   full((n, (SSD_CONV - 1) * SSD_CONV_DIM))],
        out_shape=[
            jax.ShapeDtypeStruct((n, SSD_CONV_DIM), F32),
            jax.ShapeDtypeStruct((n, SSD_INNER), F32),
            jax.ShapeDtypeStruct((n, SSD_INNER), F32),
            jax.ShapeDtypeStruct((n, (SSD_CONV - 1) * SSD_CONV_DIM), F32),
        ],
        compiler_params=_cp(("arbitrary",)),
        name="ssd_sample_prep",
    )(zx, cst, conv_w, conv_b, dtb, alog, emat)


SAMPLE_BS = 8


def _ssd_sample_state_body(st_ref, xt_ref, dct_ref, b_ref, c_ref, xs_ref, z_ref, dch_ref, nw_ref,
                           so_ref, gn_ref, y_s):
    for i in range(SAMPLE_BS):
        def group(g, carry, i=i):
            r0 = pl.multiple_of(g * 256, 256)
            h = st_ref[i, pl.ds(g * SSD_HPG, SSD_HPG)].reshape(256, SSD_STATE)
            xcol = xt_ref[pl.ds(r0, 256), i:i + 1]
            dcol = dct_ref[pl.ds(r0, 256), i:i + 1]
            brow = b_ref[i, pl.ds(g, 1), :]
            hn = h * dcol + xcol * brow
            so_ref[i, pl.ds(g * SSD_HPG, SSD_HPG)] = hn.reshape(SSD_HPG, SSD_HEAD_DIM, SSD_STATE)
            yall = _dot_x3(_split2(c_ref[g]), _split2(hn), nt=True)
            y_s[g, i:i + 1, :] = yall[i:i + 1, :]
            return carry
        lax.fori_loop(0, SSD_GROUPS, group, 0)
    for g in range(SSD_GROUPS):
        cs = slice(g * 256, (g + 1) * 256)
        y = y_s[g] + xs_ref[:, cs] * dch_ref[:, cs]
        gg = y * _silu(z_ref[:, cs])
        ms = jnp.mean(gg * gg, axis=-1, keepdims=True)
        gn_ref[:, cs] = gg * lax.rsqrt(ms + EPS) * nw_ref[:, cs]


def _ssd_sample_state(state, xt, dct, b3, c3, xc, zx, dch, nw):
    bs = SAMPLE_BS
    steps = DEC_BATCH // bs
    return pl.pallas_call(
        _ssd_sample_state_body,
        grid=(steps,),
        in_specs=[
            pl.BlockSpec((bs, SSD_HEADS, SSD_HEAD_DIM, SSD_STATE), lambda s: (s, 0, 0, 0)),
            pl.BlockSpec((None, SSD_INNER, bs), lambda s: (s, 0, 0)),
            pl.BlockSpec((None, SSD_INNER, bs), lambda s: (s, 0, 0)),
            pl.BlockSpec((bs, SSD_GROUPS, SSD_STATE), lambda s: (s, 0, 0)),
            pl.BlockSpec((SSD_GROUPS, bs, SSD_STATE), lambda s: (0, s, 0)),
            pl.BlockSpec((bs, SSD_INNER), lambda s: (s, 0)),
            pl.BlockSpec((bs, SSD_INNER), lambda s: (s, 2)),
            _resident((1, SSD_INNER)),
            _resident((1, SSD_INNER)),
        ],
        out_specs=[
            pl.BlockSpec((bs, SSD_HEADS, SSD_HEAD_DIM, SSD_STATE), lambda s: (s, 0, 0, 0)),
            pl.BlockSpec((bs, SSD_INNER), lambda s: (s, 0)),
        ],
        out_shape=[
            jax.ShapeDtypeStruct(state.shape, F32),
            jax.ShapeDtypeStruct((DEC_BATCH, SSD_INNER), F32),
        ],
        scratch_shapes=[pltpu.VMEM((SSD_GROUPS, bs, 256), F32)],
        compiler_params=_cp(("arbitrary",), 56 * 1024 * 1024),
        name="ssd_sample_state",
    )(state, xt, dct, b3, c3, xc, zx, dch, nw)


def _sconv_prompt_body(bg_ref, cg_ref, xh_ref, cw_ref, v_ref, st_ref, cbuf):
    tm = bg_ref.shape[0]
    j = pl.program_id(1)

    @pl.when(j == 0)
    def _():
        cbuf[0:8, :] = jnp.zeros((8, D_MODEL), F32)

    cbuf[8:8 + tm, :] = cg_ref[...] * xh_ref[...]
    acc = cbuf[6:6 + tm, :] * cw_ref[0:1, :]
    for k in range(1, SC_WIDTH):
        acc = acc + cbuf[6 + k:6 + k + tm, :] * cw_ref[k:k + 1, :]
    v_ref[...] = (bg_ref[...] * acc).astype(BF16)
    st_ref[...] = cbuf[6 + tm:8 + tm, :]
    cbuf[0:8, :] = cbuf[tm:tm + 8, :]


def _sconv_prompt(bcx, conv_w, tm=256):
    nt = SEQ // tm
    row = lambda b, j: b * nt + j
    return pl.pallas_call(
        _sconv_prompt_body,
        grid=(BATCH, nt),
        in_specs=[
            pl.BlockSpec((tm, D_MODEL), lambda b, j: (row(b, j), 0)),
            pl.BlockSpec((tm, D_MODEL), lambda b, j: (row(b, j), 1)),
            pl.BlockSpec((tm, D_MODEL), lambda b, j: (row(b, j), 2)),
            _resident((SC_WIDTH, D_MODEL)),
        ],
        out_specs=[
            pl.BlockSpec((tm, D_MODEL), lambda b, j: (row(b, j), 0)),
            pl.BlockSpec((None, SC_WIDTH - 1, D_MODEL), lambda b, j: (b, 0, 0)),
        ],
        out_shape=[
            jax.ShapeDtypeStruct((T_PROMPT, D_MODEL), BF16),
            jax.ShapeDtypeStruct((BATCH, SC_WIDTH - 1, D_MODEL), F32),
        ],
        scratch_shapes=[pltpu.VMEM((8 + tm, D_MODEL), F32)],
        compiler_params=_cp(("arbitrary", "arbitrary")),
        name="sconv_prompt",
    )(bcx, bcx, bcx, conv_w)


def _sconv_sample_body(bcx_ref, st_ref, cw_ref, v_ref, ns_ref):
    d = D_MODEL
    u = bcx_ref[:, d:2 * d] * bcx_ref[:, 2 * d:3 * d]
    acc = u * cw_ref[SC_WIDTH - 1:SC_WIDTH, :]
    for k in range(SC_WIDTH - 1):
        acc = acc + st_ref[:, k * d:(k + 1) * d] * cw_ref[k:k + 1, :]
    v_ref[...] = (bcx_ref[:, 0:d] * acc).astype(BF16)
    for k in range(SC_WIDTH - 2):
        ns_ref[:, k * d:(k + 1) * d] = st_ref[:, (k + 1) * d:(k + 2) * d]
    ns_ref[:, (SC_WIDTH - 2) * d:] = u


def _sconv_sample(bcx, st, conv_w):
    n = DEC_BATCH
    full = lambda shape: pl.BlockSpec(shape, lambda i: tuple(0 for _ in shape))
    return pl.pallas_call(
        _sconv_sample_body,
        grid=(1,),
        in_specs=[full((n, 3 * D_MODEL)), full((n, (SC_WIDTH - 1) * D_MODEL)), full((SC_WIDTH, D_MODEL))],
        out_specs=[full((n, D_MODEL)), full((n, (SC_WIDTH - 1) * D_MODEL))],
        out_shape=[jax.ShapeDtypeStruct((n, D_MODEL), BF16),
                   jax.ShapeDtypeStruct((n, (SC_WIDTH - 1) * D_MODEL), F32)],
        compiler_params=_cp(("arbitrary",)),
        name="sconv_sample",
    )(bcx, st, conv_w)


def _outproj_body(aliased, x_ref, a_ref, w_ref, g1_ref, nw_ref, sc_ref, sh_ref, rwt_ref, rb_ref, u_ref, cin_ref,
                  *refs):
    x1_ref, h2_ref, idx_ref, rank_ref, prr_ref, cnt_ref, run = refs[1:] if aliased else refs

    @pl.when(pl.program_id(0) == 0)
    def _():
        run[...] = cin_ref[...]

    if len(w_ref.shape) == 3:
        a_hi, a_lo = _split2(a_ref[...])
        m = _rows_x3(a_hi, jnp.concatenate([a_hi, a_lo], axis=0), w_ref, 0, D_MODEL)
    else:
        m = jnp.dot(a_ref[...], w_ref[...], preferred_element_type=F32)
    x1 = x_ref[...] + g1_ref[...] * m
    x1_ref[...] = x1
    h2 = _normmod(x1, nw_ref[...], sc_ref[...], sh_ref[...])
    _store_rows(h2_ref, h2)
    h_hi = h2.astype(BF16)
    h_lo = (h2 - h_hi.astype(F32)).astype(BF16)
    rw = rwt_ref[...]
    w_hi = rw.astype(BF16)
    w_lo = (rw - w_hi.astype(F32)).astype(BF16)
    logits = _dot_nt(w_hi, h_hi) + _dot_nt(w_hi, h_lo) + _dot_nt(w_lo, h_hi) + rb_ref[...]
    eio = lax.broadcasted_iota(jnp.int32, logits.shape, 0)
    vals, idxs = [], []
    for _ in range(TOP_K):
        mx = jnp.max(logits, axis=0, keepdims=True)
        sel = jnp.min(jnp.where(logits == mx, eio, N_EXPERTS), axis=0, keepdims=True)
        vals.append(mx)
        idxs.append(sel)
        logits = jnp.where(eio == sel, -jnp.inf, logits)
    ex = [jnp.exp(v - vals[0]) for v in vals]
    tot = ex[0] + ex[1] + ex[2] + ex[3]
    idx_ref[...] = jnp.concatenate(idxs, axis=0)
    tm = logits.shape[1]
    base = run[...]
    ranks = []
    for k in range(TOP_K):
        ohf = jnp.where(eio == idxs[k], 1.0, 0.0)
        pref = jnp.dot(ohf.astype(BF16), u_ref[...], preferred_element_type=F32)
        ranks.append(jnp.sum(ohf * (pref - 1.0 + base), axis=0, keepdims=True))
        base = base + pref[:, tm - 1:tm]
    run[...] = base
    cnt_ref[...] = base
    rank_ref[...] = jnp.concatenate(ranks, axis=0).astype(jnp.int32)
    pr = jnp.concatenate([e / tot for e in ex] + [jnp.zeros((LANES - TOP_K, tm), F32)], axis=0)
    prr_ref[...] = pr.T


def _outproj(x, a, w_bf, mod, layer, nw, rwt, rb, tm, cnt_in, h_buf, name):
    t = x.shape[0]
    k = a.shape[1]
    rows = mod.shape[2]
    tps = (t // mod.shape[1]) // tm
    aliased = h_buf is not None
    tok0 = T_PROMPT if aliased else 0
    umat = (jnp.arange(tm, dtype=jnp.int32)[:, None] <= jnp.arange(tm, dtype=jnp.int32)[None, :]).astype(BF16)
    in_specs = [
        pl.BlockSpec((tm, D_MODEL), lambda i: (i, 0)),
        pl.BlockSpec((tm, k), lambda i: (i, 0)),
        _resident(w_bf.shape),
        _mod_spec(rows, layer, 2, tps),
        _resident((1, D_MODEL)),
        _mod_spec(rows, layer, 4, tps),
        _mod_spec(rows, layer, 3, tps),
        _resident((N_EXPERTS, D_MODEL)),
        _resident((N_EXPERTS, 1)),
        _resident((tm, tm)),
        _resident((N_EXPERTS, 1)),
    ]
    args = [x, a, w_bf, mod, nw, mod, mod, rwt, rb, umat, cnt_in]
    if aliased:
        in_specs.append(pl.BlockSpec(memory_space=pl.ANY))
        args.append(h_buf)
    return pl.pallas_call(
        functools.partial(_outproj_body, aliased),
        grid=(t // tm,),
        in_specs=in_specs,
        out_specs=[
            pl.BlockSpec((tm, D_MODEL), lambda i: (i, 0)),
            pl.BlockSpec((tm * ROW_SUB, LANES), lambda i: (tok0 // tm + i, 0)),
            pl.BlockSpec((TOP_K, tm), lambda i: (0, i)),
            pl.BlockSpec((TOP_K, tm), lambda i: (0, i)),
            pl.BlockSpec((tm, LANES), lambda i: (i, 0)),
            pl.BlockSpec((N_EXPERTS, 1), lambda i: (0, 0)),
        ],
        out_shape=[
            jax.ShapeDtypeStruct((t, D_MODEL), F32),
            jax.ShapeDtypeStruct((T_ALL * ROW_SUB, LANES), F32),
            jax.ShapeDtypeStruct((TOP_K, t), jnp.int32),
            jax.ShapeDtypeStruct((TOP_K, t), jnp.int32),
            jax.ShapeDtypeStruct((t, LANES), F32),
            jax.ShapeDtypeStruct((N_EXPERTS, 1), F32),
        ],
        scratch_shapes=[pltpu.VMEM((N_EXPERTS, 1), F32)],
        input_output_aliases={len(args) - 1: 1} if aliased else {},
        compiler_params=_cp(("arbitrary",)),
        name=name,
    )(*args)


MOE_NBLK = 8


def _moe_body(te_ref, na_ref, tok_ref, tokn_ref, dstp_ref, h_hbm, wgu_ref, bgu_ref, wd_ref, bd_ref,
              perm_ref, y_hbm, xbuf, ybuf, wgu_s, wd_s, act_s, gsem, ssem):
    i = pl.program_id(0)
    last = MOE_TILES - 1
    n_active = na_ref[0]
    slot = i % 2
    oslot = 1 - slot

    unroll = 16

    def issue_rows(start_row):
        def body(j, carry):
            for u in range(unroll):
                start_row(j * unroll + u)
            return carry
        lax.fori_loop(0, MOE_TM // unroll, body, 0)

    def hbm_row(ref, idx):
        return ref.at[pl.ds(pl.multiple_of(idx, ROW_SUB), ROW_SUB)]

    def vmem_row(buf, sl, r):
        return buf.at[sl, pl.ds(pl.multiple_of(r * ROW_SUB, ROW_SUB), ROW_SUB)]

    def gather_rows(rows_ref, sl):
        issue_rows(lambda r: pltpu.make_async_copy(hbm_row(h_hbm, rows_ref[0, r]), vmem_row(xbuf, sl, r),
                                                   gsem.at[sl]).start(priority=0))

    def scatter_rows():
        issue_rows(lambda r: pltpu.make_async_copy(vmem_row(ybuf, oslot, r), hbm_row(y_hbm, dstp_ref[0, r]),
                                                   ssem.at[oslot]).start(priority=1))

    def wait_rows(buf, sem, sl):
        pltpu.make_async_copy(buf.at[sl], buf.at[sl], sem.at[sl]).wait()

    @pl.when(i == 0)
    def _():
        gather_rows(tok_ref, 0)
        ybuf[...] = jnp.zeros_like(ybuf)
        pad = (T_PAD - T_ALL) * ROW_SUB
        fills = [pltpu.make_async_copy(ybuf.at[0, pl.ds(0, pad)],
                                       y_hbm.at[pl.ds((k * T_PAD + T_ALL) * ROW_SUB, pad)], ssem.at[0])
                 for k in range(TOP_K)]
        fills += [pltpu.make_async_copy(ybuf.at[0],
                                        y_hbm.at[pl.ds((TOP_K * T_PAD + d * MOE_TM) * ROW_SUB, MOE_TM * ROW_SUB)],
                                        ssem.at[0]) for d in range(2)]
        for f in fills:
            f.start()
        for f in fills:
            f.wait()

    @pl.when(i < last)
    def _():
        gather_rows(tokn_ref, oslot)

    scatter_rows()
    wait_rows(xbuf, gsem, slot)

    @pl.when(i < n_active)
    def _():
        e_prev = te_ref[jnp.maximum(i - 1, 0)]

        @pl.when(jnp.logical_or(i == 0, e_prev != te_ref[i]))
        def _():
            for b in range(MOE_NBLK):
                cs = slice(b * 256, (b + 1) * 256)
                wgu_s[:, cs] = jnp.dot(wgu_ref[:, cs].astype(BF16), perm_ref[...],
                                       preferred_element_type=F32).astype(BF16)
            wd_s[...] = wd_ref[...].astype(BF16)

        x = _load_rows(xbuf.at[slot]).astype(BF16)
        for b in range(MOE_NBLK):
            cs = slice(b * 256, (b + 1) * 256)
            gu = jnp.dot(x, wgu_s[:, cs], preferred_element_type=F32) + bgu_ref[:, cs]
            gate = jnp.minimum(gu[:, :128], SWIGLU_LIMIT)
            up = jnp.clip(gu[:, 128:], -SWIGLU_LIMIT, SWIGLU_LIMIT)
            act_s[:, b * 128:(b + 1) * 128] = ((up + 1.0) * (gate * _sigmoid(SWIGLU_ALPHA * gate))).astype(BF16)
        y = jnp.dot(act_s[...], wd_s[...], preferred_element_type=F32) + bd_ref[...]

        @pl.when(i >= 1)
        def _():
            wait_rows(ybuf, ssem, slot)

        _store_rows(ybuf.at[slot], y)

    @pl.when(i >= n_active)
    def _():
        wait_rows(ybuf, ssem, slot)

        @pl.when(i == last)
        def _():
            wait_rows(ybuf, ssem, oslot)


def _moe_call(layer, h_all, tile_expert, n_active, row_tok, row_dst, wgu, bgu_perm, wd, bd, perm):
    smem_rows = lambda f: pl.BlockSpec((None, 1, MOE_TM), f, memory_space=pltpu.SMEM)
    grid_spec = pltpu.PrefetchScalarGridSpec(
        num_scalar_prefetch=2,
        grid=(MOE_TILES,),
        in_specs=[
            smem_rows(lambda i, te, na: (i, 0, 0)),
            smem_rows(lambda i, te, na: (jnp.minimum(i + 1, MOE_TILES - 1), 0, 0)),
            smem_rows(lambda i, te, na: (jnp.where(i == 0, MOE_TILES, i - 1), 0, 0)),
            pl.BlockSpec(memory_space=pl.ANY),
            pl.BlockSpec((None, None, D_MODEL, 2 * D_MODEL), lambda i, te, na: (layer, te[i], 0, 0)),
            pl.BlockSpec((None, None, 1, 2 * D_MODEL), lambda i, te, na: (layer, te[i], 0, 0)),
            pl.BlockSpec((None, None, D_MODEL, D_MODEL), lambda i, te, na: (layer, te[i], 0, 0)),
            pl.BlockSpec((None, None, 1, D_MODEL), lambda i, te, na: (layer, te[i], 0, 0)),
            pl.BlockSpec((256, 256), lambda i, te, na: (0, 0)),
        ],
        out_specs=pl.BlockSpec(memory_space=pl.ANY),
        scratch_shapes=[
            pltpu.VMEM((2, MOE_TM * ROW_SUB, LANES), F32),
            pltpu.VMEM((2, MOE_TM * ROW_SUB, LANES), F32),
            pltpu.VMEM((D_MODEL, 2 * D_MODEL), BF16),
            pltpu.VMEM((D_MODEL, D_MODEL), BF16),
            pltpu.VMEM((MOE_TM, D_MODEL), BF16),
            pltpu.SemaphoreType.DMA((2,)),
            pltpu.SemaphoreType.DMA((2,)),
        ],
    )
    return pl.pallas_call(
        _moe_body,
        grid_spec=grid_spec,
        out_shape=jax.ShapeDtypeStruct((Y_ROWS * ROW_SUB, LANES), F32),
        compiler_params=_cp(("arbitrary",), 56 * 1024 * 1024),
        name="moe_experts",
    )(tile_expert, n_active, row_tok, row_tok, row_dst, h_all, wgu, bgu_perm, wd, bd, perm)


def _inv_body(gs_ref, cn_ref, na_ref, dest_ref, src_ref):
    k = pl.program_id(0)

    def fill(lo, hi):
        def body(p, carry):
            src_ref[p >> 7, p & (LANES - 1)] = -1
            return carry
        lax.fori_loop(lo, hi, body, 0)

    @pl.when(k == 0)
    def _():
        end = na_ref[0] * MOE_TM
        for e in range(N_EXPERTS):
            fill(gs_ref[e] + cn_ref[e], gs_ref[e + 1] if e + 1 < N_EXPERTS else end)
        fill(end, P_PAD)

    def body(t, carry):
        d = dest_ref[t >> 7, t & (LANES - 1)]
        src_ref[d >> 7, d & (LANES - 1)] = k * T_ALL + t
        return carry
    lax.fori_loop(0, T_ALL, body, 0, unroll=8)


def _inv_call(gstart, counts, n_active, dest):
    grid_spec = pltpu.PrefetchScalarGridSpec(
        num_scalar_prefetch=3,
        grid=(TOP_K,),
        in_specs=[pl.BlockSpec((None, T_ALL // LANES, LANES), lambda k, *_: (k, 0, 0), memory_space=pltpu.SMEM)],
        out_specs=pl.BlockSpec((P_PAD // LANES, LANES), lambda k, *_: (0, 0), memory_space=pltpu.SMEM),
    )
    return pl.pallas_call(
        _inv_body,
        grid_spec=grid_spec,
        out_shape=jax.ShapeDtypeStruct((P_PAD // LANES, LANES), jnp.int32),
        compiler_params=_cp(("arbitrary",)),
        name="route_inverse",
    )(gstart, counts, n_active, dest.reshape(TOP_K, T_ALL // LANES, LANES))


def _route(idx_t, rank_t, counts_f):
    counts = counts_f.reshape(N_EXPERTS).astype(jnp.int32)
    tiles = (counts + MOE_TM - 1) // MOE_TM
    tile_end = jnp.cumsum(tiles)
    gstart = (tile_end - tiles) * MOE_TM
    n_active = tile_end[N_EXPERTS - 1]
    ti = jnp.arange(MOE_TILES, dtype=jnp.int32)
    te = jnp.sum(tile_end[None, :] <= jnp.minimum(ti, n_active - 1)[:, None], axis=1, dtype=jnp.int32)
    te = jnp.minimum(te, N_EXPERTS - 1)
    eids = jnp.arange(N_EXPERTS, dtype=jnp.int32)
    dest = rank_t + jnp.sum(jnp.where(idx_t[:, :, None] == eids, gstart, 0), axis=-1, dtype=jnp.int32)
    n_active = n_active.reshape(1)
    src = _inv_call(gstart, counts, n_active, dest).reshape(P_PAD)
    valid = src >= 0
    k = src // T_ALL
    t = src - k * T_ALL
    p = jnp.arange(P_PAD, dtype=jnp.int32)
    dump = TOP_K * T_PAD + ((p // MOE_TM) % 2) * MOE_TM + p % MOE_TM
    row_tok = (jnp.where(valid, t, 0) * ROW_SUB).reshape(MOE_TILES, 1, MOE_TM)
    row_dst = (jnp.where(valid, k * T_PAD + t, dump) * ROW_SUB).reshape(MOE_TILES, 1, MOE_TM)
    first = ((TOP_K * T_PAD + MOE_TM + jnp.arange(MOE_TM, dtype=jnp.int32)) * ROW_SUB).reshape(1, 1, MOE_TM)
    return te, n_active, row_tok, jnp.concatenate([row_dst, first], axis=0)


def _weighted(p_ref, ys):
    acc = p_ref[:, 0:1] * _load_rows(ys[0])
    for k in range(1, TOP_K):
        acc = acc + p_ref[:, k:k + 1] * _load_rows(ys[k])
    return acc


def _combine_body(x_ref, g_ref, p_ref, y0, y1, y2, y3, o_ref):
    o_ref[...] = x_ref[...] + g_ref[...] * _weighted(p_ref, (y0, y1, y2, y3))


def _combine_norm_body(x_ref, g_ref, p_ref, y0, y1, y2, y3, nw_ref, o_ref):
    x = x_ref[...] + g_ref[...] * _weighted(p_ref, (y0, y1, y2, y3))
    o_ref[...] = x * lax.rsqrt(jnp.mean(x * x, axis=-1, keepdims=True) + EPS) * nw_ref[...]


def _combine(x, mod, layer, prr, y_all, tok0, tm, final_w=None, name="combine"):
    t = x.shape[0]
    rows = mod.shape[2]
    tps = (t // mod.shape[1]) // tm
    yspec = lambda k: pl.BlockSpec((tm * ROW_SUB, LANES), lambda i: ((k * T_PAD + tok0) // tm + i, 0))
    in_specs = [pl.BlockSpec((tm, D_MODEL), lambda i: (i, 0)), _mod_spec(rows, layer, 5, tps),
                pl.BlockSpec((tm, LANES), lambda i: (i, 0)), yspec(0), yspec(1), yspec(2), yspec(3)]
    args = [x, mod, prr, y_all, y_all, y_all, y_all]
    body = _combine_body
    if final_w is not None:
        in_specs.append(_resident((1, D_MODEL)))
        args.append(final_w)
        body = _combine_norm_body
    return pl.pallas_call(
        body,
        grid=(t // tm,),
        in_specs=in_specs,
        out_specs=pl.BlockSpec((tm, D_MODEL), lambda i: (i, 0)),
        out_shape=jax.ShapeDtypeStruct((t, D_MODEL), F32),
        compiler_params=_cp(("arbitrary",)),
        name=name,
    )(*args)


def _mixer_out_and_moe(layer, x_p, a_p, x_s, a_s, w_out, mod_p, mod_s, nw, rwt, rb, tm_p, tm_s,
                       moe_w_gate_up, bgu_perm, moe_w_down, moe_b_down, perm):
    zero_cnt = jnp.zeros((N_EXPERTS, 1), F32)
    x1_p, h_all, idx_p, rank_p, prr_p, cnt = _outproj(x_p, a_p, w_out, mod_p, layer, nw, rwt, rb, tm_p, zero_cnt,
                                                      None, "outproj%d_prompt" % layer)
    x1_s, h_all, idx_s, rank_s, prr_s, cnt = _outproj(x_s, a_s, w_out, mod_s, layer, nw, rwt, rb, tm_s, cnt,
                                                      h_all, "outproj%d_sample" % layer)
    idx_t = jnp.concatenate([idx_p, idx_s], axis=1)
    rank_t = jnp.concatenate([rank_p, rank_s], axis=1)
    te, n_active, row_tok, row_dst = _route(idx_t, rank_t, cnt)
    y_all = _moe_call(layer, h_all, te, n_active, row_tok, row_dst, moe_w_gate_up, bgu_perm,
                      moe_w_down, moe_b_down.reshape(-1, N_EXPERTS, 1, D_MODEL), perm)
    return x1_p, x1_s, prr_p, prr_s, y_all


def kernel(x_prompt, x_sample, c_prompt, c_sample, state_ssm, state_ssm_conv, state_sconv, ada_w, ada_b, norm1_w, norm2_w, ssd_w_in, ssd_conv_w, ssd_conv_b, ssd_dt_bias, ssd_A_log, ssd_D, ssd_norm_w, ssd_w_out, sc_w_in, sc_conv_w, sc_w_out, router_w, router_b, moe_w_gate_up, moe_b_gate_up, moe_w_down, moe_b_down, final_norm_w):
    tm_p = 256
    tm_s = DEC_BATCH
    depth = ada_w.shape[0]

    def hilo(w):
        hi = w.astype(BF16)
        return jnp.stack([hi, (w - hi.astype(F32)).astype(BF16)])

    w_in0 = hilo(jnp.concatenate(
        [ssd_w_in[:, SSD_INNER:SSD_INNER + SSD_CONV_DIM], ssd_w_in[:, :SSD_INNER],
         ssd_w_in[:, SSD_INNER + SSD_CONV_DIM:], jnp.zeros((D_MODEL, HEADS_PAD - SSD_HEADS), F32)],
        axis=1))
    w_out0 = hilo(ssd_w_out)
    w_in1 = sc_w_in.astype(BF16)
    w_out1 = sc_w_out.astype(BF16)
    pad_h = lambda v: jnp.concatenate([v, jnp.zeros((HEADS_PAD - SSD_HEADS,), F32)]).reshape(1, HEADS_PAD)
    dtb = pad_h(ssd_dt_bias)
    alog = pad_h(ssd_A_log)
    dch = jnp.repeat(ssd_D, SSD_HEAD_DIM).reshape(1, SSD_INNER)
    ssd_nw = ssd_norm_w.reshape(1, SSD_INNER)
    conv_b = ssd_conv_b.reshape(1, SSD_CONV_DIM)
    emat = (jnp.arange(SSD_INNER, dtype=jnp.int32)[None, :] // SSD_HEAD_DIM
            == jnp.arange(HEADS_PAD, dtype=jnp.int32)[:, None]).astype(BF16)
    jj = jnp.arange(256, dtype=jnp.int32)
    perm = (jj[None, :] == jnp.where(jj % 2 == 0, jj // 2, 128 + jj // 2)[:, None]).astype(BF16)
    bgu_perm = moe_b_gate_up.reshape(depth, N_EXPERTS, 8, 128, 2).transpose(0, 1, 2, 4, 3).reshape(
        depth, N_EXPERTS, 1, 2 * D_MODEL)
    rwt = jnp.transpose(router_w, (0, 2, 1))
    rb = router_b.reshape(depth, N_EXPERTS, 1)
    n1 = norm1_w.reshape(depth, 1, D_MODEL)
    n2 = norm2_w.reshape(depth, 1, D_MODEL)

    mod = _ada_call(jnp.concatenate([c_prompt, c_sample], axis=0), ada_w, ada_b)
    mod_p = mod[:, :BATCH].reshape(depth, BATCH, 1, 6 * D_MODEL)
    mod_s = mod[:, BATCH:].reshape(depth, 1, DEC_BATCH, 6 * D_MODEL)

    xp = x_prompt.reshape(T_PROMPT, D_MODEL)
    xs = x_sample.reshape(DEC_BATCH, D_MODEL)

    zx_p = _normmod_matmul(xp, n1[0], mod_p, 0, 1, 0, w_in0, tm_p, "inproj0_prompt")
    gn_p, ssm_p, ssm_conv_p = _ssd_prompt(zx_p, ssd_conv_w, conv_b, dtb, alog, dch, ssd_nw, emat)
    zx_s = _normmod_matmul(xs, n1[0], mod_s, 0, 1, 0, w_in0, tm_s, "inproj0_sample")
    xc_s, xdt_s, dec_s, ncs_s = _ssd_sample_prep(
        zx_s, state_ssm_conv.reshape(DEC_BATCH, (SSD_CONV - 1) * SSD_CONV_DIM), ssd_conv_w, conv_b, dtb, alog, emat)
    steps = DEC_BATCH // SAMPLE_BS
    to_cols = lambda a: a.reshape(steps, SAMPLE_BS, SSD_INNER).transpose(0, 2, 1)
    b3 = xc_s[:, SSD_INNER:SSD_INNER + SSD_GN].reshape(DEC_BATCH, SSD_GROUPS, SSD_STATE)
    c3 = xc_s[:, SSD_INNER + SSD_GN:].reshape(DEC_BATCH, SSD_GROUPS, SSD_STATE).transpose(1, 0, 2)
    ssm_s, gn_s = _ssd_sample_state(state_ssm, to_cols(xdt_s), to_cols(dec_s), b3, c3, xc_s, zx_s, dch, ssd_nw)
    ssm_conv_s = ncs_s.reshape(DEC_BATCH, SSD_CONV - 1, SSD_CONV_DIM)

    x1_p, x1_s, prr_p, prr_s, y0 = _mixer_out_and_moe(
        0, xp, gn_p, xs, gn_s, w_out0, mod_p, mod_s, n2[0], rwt[0], rb[0], tm_p, tm_s,
        moe_w_gate_up, bgu_perm, moe_w_down, moe_b_down, perm)
    x2_p = _combine(x1_p, mod_p, 0, prr_p, y0, 0, tm_p, name="combine0_prompt")
    x2_s = _combine(x1_s, mod_s, 0, prr_s, y0, T_PROMPT, tm_s, name="combine0_sample")

    bcx_p = _normmod_matmul(x2_p, n1[1], mod_p, 1, 1, 0, w_in1, tm_p, "inproj1_prompt")
    v_p, sconv_p = _sconv_prompt(bcx_p, sc_conv_w)
    bcx_s = _normmod_matmul(x2_s, n1[1], mod_s, 1, 1, 0, w_in1, tm_s, "inproj1_sample")
    v_s, nsc_s = _sconv_sample(bcx_s, state_sconv.reshape(DEC_BATCH, (SC_WIDTH - 1) * D_MODEL), sc_conv_w)
    sconv_s = nsc_s.reshape(DEC_BATCH, SC_WIDTH - 1, D_MODEL)

    x3_p, x3_s, prr_p, prr_s, y1 = _mixer_out_and_moe(
        1, x2_p, v_p, x2_s, v_s, w_out1, mod_p, mod_s, n2[1], rwt[1], rb[1], tm_p, tm_s,
        moe_w_gate_up, bgu_perm, moe_w_down, moe_b_down, perm)
    fw = final_norm_w.reshape(1, D_MODEL)
    y_p = _combine(x3_p, mod_p, 1, prr_p, y1, 0, tm_p, final_w=fw, name="final_prompt")
    y_s = _combine(x3_s, mod_s, 1, prr_s, y1, T_PROMPT, tm_s, final_w=fw, name="final_sample")

    return (y_p.reshape(BATCH, SEQ, D_MODEL), y_s.reshape(DEC_BATCH, 1, D_MODEL), ssm_p, ssm_conv_p, sconv_p,
            ssm_s, ssm_conv_s, sconv_s)
```

```python
import functools

import jax
import jax.numpy as jnp
from jax import lax
from jax.experimental import pallas as pl
from jax.experimental.pallas import tpu as pltpu

F32 = jnp.float32
BF16 = jnp.bfloat16

D_MODEL = 1024
BATCH = 8
SEQ = 2048
DEC_BATCH = 128
SSD_INNER = 2048
SSD_HEAD_DIM = 64
SSD_HEADS = 32
SSD_GROUPS = 8
SSD_HPG = 4
SSD_STATE = 128
SSD_CONV = 4
SSD_CHUNK = 128
SSD_GN = 1024
SSD_CONV_DIM = 4096
SC_WIDTH = 3
N_EXPERTS = 32
TOP_K = 4
SWIGLU_LIMIT = 7.0
SWIGLU_ALPHA = 1.702
EPS = 1e-5

LANES = 128
HEADS_PAD = LANES
ZX_COLS = SSD_CONV_DIM + SSD_INNER + HEADS_PAD
T_PROMPT = BATCH * SEQ
T_ALL = T_PROMPT + DEC_BATCH
MOE_TM = 256
T_PAD = 16640
N_ASSIGN = TOP_K * T_ALL
MOE_TILES = (N_ASSIGN + N_EXPERTS * (MOE_TM - 1)) // MOE_TM + 1
P_PAD = MOE_TILES * MOE_TM
Y_ROWS = TOP_K * T_PAD + 2 * MOE_TM
VMEM_LIMIT = 48 * 1024 * 1024


def _cp(sem, vmem=VMEM_LIMIT):
    return pltpu.CompilerParams(dimension_semantics=sem, vmem_limit_bytes=vmem)


def _sigmoid(x):
    return 1.0 / (1.0 + jnp.exp(-x))


def _silu(x):
    return x * _sigmoid(x)


def _softplus(x):
    return jnp.maximum(x, 0.0) + jnp.log1p(jnp.exp(-jnp.abs(x)))


def _normmod(x, nw, sc, sh):
    y = x * lax.rsqrt(jnp.mean(x * x, axis=-1, keepdims=True) + EPS)
    return y * nw * (1.0 + sc) + sh


def _split3(x):
    hi = x.astype(BF16)
    r1 = x - hi.astype(F32)
    mid = r1.astype(BF16)
    lo = (r1 - mid.astype(F32)).astype(BF16)
    return hi, mid, lo


def _dot_exact_lhs(x, m01):
    hi, mid, lo = _split3(x)
    d = functools.partial(jnp.dot, preferred_element_type=F32)
    return d(hi, m01) + d(mid, m01) + d(lo, m01)


def _dot_exact_rhs(m01, x):
    hi, mid, lo = _split3(x)
    d = functools.partial(jnp.dot, preferred_element_type=F32)
    return d(m01, hi) + d(m01, mid) + d(m01, lo)


def _dot_nt(a, b):
    return lax.dot_general(a, b, (((1,), (1,)), ((), ())), preferred_element_type=F32)


def _split2(x):
    hi = x.astype(BF16)
    return hi, (x - hi.astype(F32)).astype(BF16)


def _dot_x3(a, b, nt=False):
    d = _dot_nt if nt else functools.partial(jnp.dot, preferred_element_type=F32)
    return d(a[0], b[0]) + d(a[1], b[0]) + d(a[0], b[1])


ROW_SUB = D_MODEL // LANES


def _load_rows(ref):
    tm = ref.shape[0] // ROW_SUB
    return jnp.concatenate([ref[pl.ds(c, tm, stride=ROW_SUB), :] for c in range(ROW_SUB)], axis=1)


def _store_rows(ref, x):
    tm = ref.shape[0] // ROW_SUB
    for c in range(ROW_SUB):
        ref[pl.ds(c, tm, stride=ROW_SUB), :] = x[:, c * LANES:(c + 1) * LANES]


def _ada_body(c_ref, w_ref, b_ref, o_ref):
    o_ref[...] = _dot_x3(_split2(_silu(c_ref[...])), _split2(w_ref[...])) + b_ref[...]


def _ada_call(c_all, ada_w, ada_b):
    n = c_all.shape[0]
    depth = ada_w.shape[0]
    return pl.pallas_call(
        _ada_body,
        grid=(depth, 6),
        in_specs=[
            pl.BlockSpec((n, D_MODEL), lambda l, j: (0, 0)),
            pl.BlockSpec((None, D_MODEL, D_MODEL), lambda l, j: (l, 0, j)),
            pl.BlockSpec((None, 1, D_MODEL), lambda l, j: (l, 0, j)),
        ],
        out_specs=pl.BlockSpec((None, n, D_MODEL), lambda l, j: (l, 0, j)),
        out_shape=jax.ShapeDtypeStruct((depth, n, 6 * D_MODEL), F32),
        compiler_params=_cp(("arbitrary", "arbitrary")),
        name="ada_mod",
    )(c_all, ada_w, ada_b.reshape(depth, 1, 6 * D_MODEL))


def _col_chunks(n, step=512):
    return [(s, min(step, n - s)) for s in range(0, n, step)]


def _rows_x3(a_hi, a_hilo, w_ref, s, n):
    tm = a_hi.shape[0]
    r = jnp.dot(a_hilo, w_ref[0, :, s:s + n], preferred_element_type=F32)
    return r[:tm] + r[tm:] + jnp.dot(a_hi, w_ref[1, :, s:s + n], preferred_element_type=F32)


def _nm_body(x_ref, nw_ref, sc_ref, sh_ref, w_ref, o_ref):
    h = _normmod(x_ref[...], nw_ref[...], sc_ref[...], sh_ref[...])
    if len(w_ref.shape) == 3:
        h_hi, h_lo = _split2(h)
        h_hilo = jnp.concatenate([h_hi, h_lo], axis=0)
        for s, n in _col_chunks(o_ref.shape[1]):
            o_ref[:, s:s + n] = _rows_x3(h_hi, h_hilo, w_ref, s, n)
    else:
        h = h.astype(BF16)
        for s, n in _col_chunks(o_ref.shape[1]):
            o_ref[:, s:s + n] = jnp.dot(h, w_ref[:, s:s + n], preferred_element_type=F32)


def _mod_spec(rows, layer, col, tiles_per_seq):
    return pl.BlockSpec((None, None, rows, D_MODEL), lambda i: (layer, i // tiles_per_seq, 0, col))


def _resident(shape):
    return pl.BlockSpec(shape, lambda *_: tuple(0 for _ in shape), pipeline_mode=pl.Buffered(1))


def _normmod_matmul(x, nw, mod, layer, sc_col, sh_col, w_bf, tm, name):
    t = x.shape[0]
    n = w_bf.shape[-1]
    rows = mod.shape[2]
    tps = (t // mod.shape[1]) // tm
    return pl.pallas_call(
        _nm_body,
        grid=(t // tm,),
        in_specs=[
            pl.BlockSpec((tm, D_MODEL), lambda i: (i, 0)),
            _resident((1, D_MODEL)),
            _mod_spec(rows, layer, sc_col, tps),
            _mod_spec(rows, layer, sh_col, tps),
            _resident(w_bf.shape),
        ],
        out_specs=pl.BlockSpec((tm, n), lambda i: (i, 0)),
        out_shape=jax.ShapeDtypeStruct((t, n), F32),
        compiler_params=_cp(("arbitrary",)),
        name=name,
    )(x, nw, mod, mod, w_bf)


def _ssd_prompt_body(xbc_ref, z_ref, dt_ref, cw_ref, cb_ref, dtb_ref, alog_ref, dch_ref, nw_ref, e_ref,
                     gn_ref, st_ref, cs_ref, cbuf, ht, xc_s, yd_s):
    q = SSD_CHUNK
    c = pl.program_id(1)

    @pl.when(c == 0)
    def _():
        cbuf[0:8, :] = jnp.zeros((8, SSD_CONV_DIM), F32)
        ht[...] = jnp.zeros_like(ht)

    cbuf[8:8 + q, :] = xbc_ref[...]
    for s, n in _col_chunks(SSD_CONV_DIM):
        acc = cbuf[5:5 + q, s:s + n] * cw_ref[0:1, s:s + n]
        for k in range(1, SSD_CONV):
            acc = acc + cbuf[5 + k:5 + k + q, s:s + n] * cw_ref[k:k + 1, s:s + n]
        xc_s[:, s:s + n] = _silu(acc + cb_ref[:, s:s + n])
    cs_ref[...] = cbuf[5 + q:8 + q, :]
    cbuf[0:8, :] = cbuf[q:q + 8, :]

    dt = _softplus(dt_ref[...] + dtb_ref[...])
    da = dt * (-jnp.exp(alog_ref[...]))
    ri = lax.broadcasted_iota(jnp.int32, (q, q), 0)
    ci = lax.broadcasted_iota(jnp.int32, (q, q), 1)
    causal = ri >= ci
    tri = jnp.where(causal, 1.0, 0.0).astype(BF16)
    tri_t = jnp.where(ri <= ci, 1.0, 0.0).astype(BF16)
    acs = _dot_exact_rhs(tri, da)
    acs_t = _dot_exact_lhs(da.T, tri_t)

    for g in range(SSD_GROUPS):
        cs = slice(g * 256, (g + 1) * 256)
        e_g = e_ref[:, cs]
        dte = _dot_exact_lhs(dt, e_g)
        ace = _dot_exact_lhs(acs, e_g)
        last = ace[q - 1:q, :]
        xs_g = xc_s[:, cs]
        xdt = xs_g * dte
        b_g = xc_s[:, SSD_INNER + g * SSD_STATE:SSD_INNER + (g + 1) * SSD_STATE]
        c_g = xc_s[:, SSD_INNER + SSD_GN + g * SSD_STATE:SSD_INNER + SSD_GN + (g + 1) * SSD_STATE]
        c_2 = _split2(c_g)
        cb = _dot_x3(c_2, _split2(b_g), nt=True)
        h_prev = ht[g]
        y_off = _dot_x3(c_2, _split2(h_prev)) * jnp.exp(ace)
        st_t = _dot_x3(_split2(b_g.T), _split2(jnp.exp(last - ace) * xdt))
        ht[g] = jnp.exp(last) * h_prev + st_t
        xdt_hi, xdt_lo = _split2(xdt)
        for r in range(SSD_HPG):
            h = g * SSD_HPG + r
            hs = slice(r * 64, (r + 1) * 64)
            diff = acs[:, h:h + 1] - acs_t[h:h + 1, :]
            m_2 = _split2(cb * jnp.where(causal, jnp.exp(diff), 0.0))
            yd_s[:, hs] = _dot_x3(m_2, (xdt_hi[:, hs], xdt_lo[:, hs]))
        y = yd_s[...] + y_off + xs_g * dch_ref[:, cs]
        gg = y * _silu(z_ref[:, cs])
        ms = jnp.mean(gg * gg, axis=-1, keepdims=True)
        gn_ref[:, cs] = gg * lax.rsqrt(ms + EPS) * nw_ref[:, cs]

    @pl.when(c == pl.num_programs(1) - 1)
    def _():
        for g in range(SSD_GROUPS):
            t = ht[g].T
            for r in range(SSD_HPG):
                st_ref[g * SSD_HPG + r] = t[r * 64:(r + 1) * 64, :]


def _ssd_prompt(zx, conv_w, conv_b, dtb, alog, dch, nw, emat):
    nc = SEQ // SSD_CHUNK
    q = SSD_CHUNK
    row = lambda b, c: b * nc + c
    return pl.pallas_call(
        _ssd_prompt_body,
        grid=(BATCH, nc),
        in_specs=[
            pl.BlockSpec((q, SSD_CONV_DIM), lambda b, c: (row(b, c), 0)),
            pl.BlockSpec((q, SSD_INNER), lambda b, c: (row(b, c), 2)),
            pl.BlockSpec((q, HEADS_PAD), lambda b, c: (row(b, c), (SSD_CONV_DIM + SSD_INNER) // HEADS_PAD)),
            _resident((SSD_CONV, SSD_CONV_DIM)),
            _resident((1, SSD_CONV_DIM)),
            _resident((1, HEADS_PAD)),
            _resident((1, HEADS_PAD)),
            _resident((1, SSD_INNER)),
            _resident((1, SSD_INNER)),
            _resident((HEADS_PAD, SSD_INNER)),
        ],
        out_specs=[
            pl.BlockSpec((q, SSD_INNER), lambda b, c: (row(b, c), 0)),
            pl.BlockSpec((None, SSD_HEADS, SSD_HEAD_DIM, SSD_STATE), lambda b, c: (b, 0, 0, 0)),
            pl.BlockSpec((None, SSD_CONV - 1, SSD_CONV_DIM), lambda b, c: (b, 0, 0)),
        ],
        out_shape=[
            jax.ShapeDtypeStruct((T_PROMPT, SSD_INNER), F32),
            jax.ShapeDtypeStruct((BATCH, SSD_HEADS, SSD_HEAD_DIM, SSD_STATE), F32),
            jax.ShapeDtypeStruct((BATCH, SSD_CONV - 1, SSD_CONV_DIM), F32),
        ],
        scratch_shapes=[
            pltpu.VMEM((8 + q, SSD_CONV_DIM), F32),
            pltpu.VMEM((SSD_GROUPS, SSD_STATE, 256), F32),
            pltpu.VMEM((q, SSD_CONV_DIM), F32),
            pltpu.VMEM((q, 256), F32),
        ],
        compiler_params=_cp(("arbitrary", "arbitrary")),
        name="ssd_prompt",
    )(zx, zx, zx, conv_w, conv_b, dtb, alog, dch, nw, emat)


def _ssd_sample_prep_body(zx_ref, cst_ref, cw_ref, cb_ref, dtb_ref, alog_ref, e_ref,
                          xc_ref, xdt_ref, dec_ref, ncs_ref):
    kc = SSD_CONV_DIM
    for s, n in _col_chunks(kc):
        xnew = zx_ref[:, s:s + n]
        acc = xnew * cw_ref[SSD_CONV - 1:SSD_CONV, s:s + n]
        for k in range(SSD_CONV - 1):
            acc = acc + cst_ref[:, k * kc + s:k * kc + s + n] * cw_ref[k:k + 1, s:s + n]
        xc_ref[:, s:s + n] = _silu(acc + cb_ref[:, s:s + n])
        for k in range(SSD_CONV - 2):
            ncs_ref[:, k * kc + s:k * kc + s + n] = cst_ref[:, (k + 1) * kc + s:(k + 1) * kc + s + n]
        ncs_ref[:, (SSD_CONV - 2) * kc + s:(SSD_CONV - 2) * kc + s + n] = xnew
    dt = _softplus(zx_ref[:, SSD_CONV_DIM + SSD_INNER:] + dtb_ref[...])
    dec_ref[...] = jnp.exp(dt * (-jnp.exp(alog_ref[...])))
    for s, n in _col_chunks(SSD_INNER):
        xdt_ref[:, s:s + n] = xc_ref[:, s:s + n] * _dot_exact_lhs(dt, e_ref[:, s:s + n])


def _ssd_sample_prep(zx, cst, conv_w, conv_b, dtb, alog, emat):
    n = DEC_BATCH
    full = lambda shape: pl.BlockSpec(shape, lambda i: tuple(0 for _ in shape))
    return pl.pallas_call(
        _ssd_sample_prep_body,
        grid=(1,),
        in_specs=[full((n, ZX_COLS)), full((n, (SSD_CONV - 1) * SSD_CONV_DIM)), full((SSD_CONV, SSD_CONV_DIM)),
                  full((1, SSD_CONV_DIM)), full((1, HEADS_PAD)), full((1, HEADS_PAD)),
                  full((HEADS_PAD, SSD_INNER))],
        out_specs=[full((n, SSD_CONV_DIM)), full((n, SSD_INNER)), full((n, HEADS_PAD)),
                   full((n, (SSD_CONV - 1) * SSD_CONV_DIM))],
        out_shape=[
            jax.ShapeDtypeStruct((n, SSD_CONV_DIM), F32),
            jax.ShapeDtypeStruct((n, SSD_INNER), F32),
            jax.ShapeDtypeStruct((n, HEADS_PAD), F32),
            jax.ShapeDtypeStruct((n, (SSD_CONV - 1) * SSD_CONV_DIM), F32),
        ],
        compiler_params=_cp(("arbitrary",)),
        name="ssd_sample_prep",
    )(zx, cst, conv_w, conv_b, dtb, alog, emat)


SAMPLE_BS = 8


def _ssd_sample_state_body(dec_ref, st_ref, xt_ref, b_ref, c_ref, xs_ref, z_ref, dch_ref, nw_ref,
                           so_ref, gn_ref, y_s):
    def group(g, carry):
        r0 = pl.multiple_of(g * 256, 256)
        c_2 = _split2(c_ref[g])
        for i in range(SAMPLE_BS):
            h = st_ref[i, pl.ds(g * SSD_HPG, SSD_HPG)]
            hd = jnp.concatenate([h[r] * dec_ref[i, g * SSD_HPG + r] for r in range(SSD_HPG)], axis=0)
            xcol = xt_ref[pl.ds(r0, 256), i:i + 1]
            brow = b_ref[i, pl.ds(g, 1), :]
            hn = hd + xcol * brow
            so_ref[i, pl.ds(g * SSD_HPG, SSD_HPG)] = hn.reshape(SSD_HPG, SSD_HEAD_DIM, SSD_STATE)
            yall = _dot_x3(c_2, _split2(hn), nt=True)
            y_s[g, i:i + 1, :] = yall[i:i + 1, :]
        return carry
    lax.fori_loop(0, SSD_GROUPS, group, 0)
    for g in range(SSD_GROUPS):
        cs = slice(g * 256, (g + 1) * 256)
        y = y_s[g] + xs_ref[:, cs] * dch_ref[:, cs]
        gg = y * _silu(z_ref[:, cs])
        ms = jnp.mean(gg * gg, axis=-1, keepdims=True)
        gn_ref[:, cs] = gg * lax.rsqrt(ms + EPS) * nw_ref[:, cs]


def _ssd_sample_state(state, xt, dec, b3, c3, xc, zx, dch, nw):
    bs = SAMPLE_BS
    steps = DEC_BATCH // bs
    return pl.pallas_call(
        _ssd_sample_state_body,
        grid=(steps,),
        in_specs=[
            pl.BlockSpec((None, bs, SSD_HEADS), lambda s: (s, 0, 0), memory_space=pltpu.SMEM),
            pl.BlockSpec((bs, SSD_HEADS, SSD_HEAD_DIM, SSD_STATE), lambda s: (s, 0, 0, 0)),
            pl.BlockSpec((None, SSD_INNER, bs), lambda s: (s, 0, 0)),
            pl.BlockSpec((bs, SSD_GROUPS, SSD_STATE), lambda s: (s, 0, 0)),
            pl.BlockSpec((SSD_GROUPS, bs, SSD_STATE), lambda s: (0, s, 0)),
            pl.BlockSpec((bs, SSD_INNER), lambda s: (s, 0)),
            pl.BlockSpec((bs, SSD_INNER), lambda s: (s, 2)),
            _resident((1, SSD_INNER)),
            _resident((1, SSD_INNER)),
        ],
        out_specs=[
            pl.BlockSpec((bs, SSD_HEADS, SSD_HEAD_DIM, SSD_STATE), lambda s: (s, 0, 0, 0)),
            pl.BlockSpec((bs, SSD_INNER), lambda s: (s, 0)),
        ],
        out_shape=[
            jax.ShapeDtypeStruct(state.shape, F32),
            jax.ShapeDtypeStruct((DEC_BATCH, SSD_INNER), F32),
        ],
        scratch_shapes=[pltpu.VMEM((SSD_GROUPS, bs, 256), F32)],
        compiler_params=_cp(("arbitrary",), 56 * 1024 * 1024),
        name="ssd_sample_state",
    )(dec[:, :SSD_HEADS].reshape(steps, bs, SSD_HEADS), state, xt, b3, c3, xc, zx, dch, nw)


def _sconv_prompt_body(bg_ref, cg_ref, xh_ref, cw_ref, v_ref, st_ref, cbuf):
    tm = bg_ref.shape[0]
    j = pl.program_id(1)

    @pl.when(j == 0)
    def _():
        cbuf[0:8, :] = jnp.zeros((8, D_MODEL), F32)

    cbuf[8:8 + tm, :] = cg_ref[...] * xh_ref[...]
    acc = cbuf[6:6 + tm, :] * cw_ref[0:1, :]
    for k in range(1, SC_WIDTH):
        acc = acc + cbuf[6 + k:6 + k + tm, :] * cw_ref[k:k + 1, :]
    v_ref[...] = (bg_ref[...] * acc).astype(BF16)
    st_ref[...] = cbuf[6 + tm:8 + tm, :]
    cbuf[0:8, :] = cbuf[tm:tm + 8, :]


def _sconv_prompt(bcx, conv_w, tm=256):
    nt = SEQ // tm
    row = lambda b, j: b * nt + j
    return pl.pallas_call(
        _sconv_prompt_body,
        grid=(BATCH, nt),
        in_specs=[
            pl.BlockSpec((tm, D_MODEL), lambda b, j: (row(b, j), 0)),
            pl.BlockSpec((tm, D_MODEL), lambda b, j: (row(b, j), 1)),
            pl.BlockSpec((tm, D_MODEL), lambda b, j: (row(b, j), 2)),
            _resident((SC_WIDTH, D_MODEL)),
        ],
        out_specs=[
            pl.BlockSpec((tm, D_MODEL), lambda b, j: (row(b, j), 0)),
            pl.BlockSpec((None, SC_WIDTH - 1, D_MODEL), lambda b, j: (b, 0, 0)),
        ],
        out_shape=[
            jax.ShapeDtypeStruct((T_PROMPT, D_MODEL), BF16),
            jax.ShapeDtypeStruct((BATCH, SC_WIDTH - 1, D_MODEL), F32),
        ],
        scratch_shapes=[pltpu.VMEM((8 + tm, D_MODEL), F32)],
        compiler_params=_cp(("arbitrary", "arbitrary")),
        name="sconv_prompt",
    )(bcx, bcx, bcx, conv_w)


def _sconv_sample_body(bcx_ref, st_ref, cw_ref, v_ref, ns_ref):
    d = D_MODEL
    u = bcx_ref[:, d:2 * d] * bcx_ref[:, 2 * d:3 * d]
    acc = u * cw_ref[SC_WIDTH - 1:SC_WIDTH, :]
    for k in range(SC_WIDTH - 1):
        acc = acc + st_ref[:, k * d:(k + 1) * d] * cw_ref[k:k + 1, :]
    v_ref[...] = (bcx_ref[:, 0:d] * acc).astype(BF16)
    for k in range(SC_WIDTH - 2):
        ns_ref[:, k * d:(k + 1) * d] = st_ref[:, (k + 1) * d:(k + 2) * d]
    ns_ref[:, (SC_WIDTH - 2) * d:] = u


def _sconv_sample(bcx, st, conv_w):
    n = DEC_BATCH
    full = lambda shape: pl.BlockSpec(shape, lambda i: tuple(0 for _ in shape))
    return pl.pallas_call(
        _sconv_sample_body,
        grid=(1,),
        in_specs=[full((n, 3 * D_MODEL)), full((n, (SC_WIDTH - 1) * D_MODEL)), full((SC_WIDTH, D_MODEL))],
        out_specs=[full((n, D_MODEL)), full((n, (SC_WIDTH - 1) * D_MODEL))],
        out_shape=[jax.ShapeDtypeStruct((n, D_MODEL), BF16),
                   jax.ShapeDtypeStruct((n, (SC_WIDTH - 1) * D_MODEL), F32)],
        compiler_params=_cp(("arbitrary",)),
        name="sconv_sample",
    )(bcx, st, conv_w)


def _outproj_body(aliased, x_ref, a_ref, w_ref, g1_ref, nw_ref, sc_ref, sh_ref, rwt_ref, rb_ref, u_ref, cin_ref,
                  *refs):
    x1_ref, h2_ref, idx_ref, rank_ref, prr_ref, cnt_ref, run = refs[1:] if aliased else refs

    @pl.when(pl.program_id(0) == 0)
    def _():
        run[...] = cin_ref[...]

    if len(w_ref.shape) == 3:
        a_hi, a_lo = _split2(a_ref[...])
        m = _rows_x3(a_hi, jnp.concatenate([a_hi, a_lo], axis=0), w_ref, 0, D_MODEL)
    else:
        m = jnp.dot(a_ref[...], w_ref[...], preferred_element_type=F32)
    x1 = x_ref[...] + g1_ref[...] * m
    x1_ref[...] = x1
    h2 = _normmod(x1, nw_ref[...], sc_ref[...], sh_ref[...])
    _store_rows(h2_ref, h2)
    h_hi = h2.astype(BF16)
    h_lo = (h2 - h_hi.astype(F32)).astype(BF16)
    rw = rwt_ref[...]
    w_hi = rw.astype(BF16)
    w_lo = (rw - w_hi.astype(F32)).astype(BF16)
    logits = _dot_nt(w_hi, h_hi) + _dot_nt(w_hi, h_lo) + _dot_nt(w_lo, h_hi) + rb_ref[...]
    eio = lax.broadcasted_iota(jnp.int32, logits.shape, 0)
    vals, idxs = [], []
    for _ in range(TOP_K):
        mx = jnp.max(logits, axis=0, keepdims=True)
        sel = jnp.min(jnp.where(logits == mx, eio, N_EXPERTS), axis=0, keepdims=True)
        vals.append(mx)
        idxs.append(sel)
        logits = jnp.where(eio == sel, -jnp.inf, logits)
    ex = [jnp.exp(v - vals[0]) for v in vals]
    tot = ex[0] + ex[1] + ex[2] + ex[3]
    idx_ref[...] = jnp.concatenate(idxs, axis=0)
    tm = logits.shape[1]
    base = run[...]
    ranks = []
    for k in range(TOP_K):
        ohf = jnp.where(eio == idxs[k], 1.0, 0.0)
        pref = jnp.dot(ohf.astype(BF16), u_ref[...], preferred_element_type=F32)
        ranks.append(jnp.sum(ohf * (pref - 1.0 + base), axis=0, keepdims=True))
        base = base + pref[:, tm - 1:tm]
    run[...] = base
    cnt_ref[...] = base
    rank_ref[...] = jnp.concatenate(ranks, axis=0).astype(jnp.int32)
    pr = jnp.concatenate([e / tot for e in ex] + [jnp.zeros((LANES - TOP_K, tm), F32)], axis=0)
    prr_ref[...] = pr.T


def _outproj(x, a, w_bf, mod, layer, nw, rwt, rb, tm, cnt_in, h_buf, name):
    t = x.shape[0]
    k = a.shape[1]
    rows = mod.shape[2]
    tps = (t // mod.shape[1]) // tm
    aliased = h_buf is not None
    tok0 = T_PROMPT if aliased else 0
    umat = (jnp.arange(tm, dtype=jnp.int32)[:, None] <= jnp.arange(tm, dtype=jnp.int32)[None, :]).astype(BF16)
    in_specs = [
        pl.BlockSpec((tm, D_MODEL), lambda i: (i, 0)),
        pl.BlockSpec((tm, k), lambda i: (i, 0)),
        _resident(w_bf.shape),
        _mod_spec(rows, layer, 2, tps),
        _resident((1, D_MODEL)),
        _mod_spec(rows, layer, 4, tps),
        _mod_spec(rows, layer, 3, tps),
        _resident((N_EXPERTS, D_MODEL)),
        _resident((N_EXPERTS, 1)),
        _resident((tm, tm)),
        _resident((N_EXPERTS, 1)),
    ]
    args = [x, a, w_bf, mod, nw, mod, mod, rwt, rb, umat, cnt_in]
    if aliased:
        in_specs.append(pl.BlockSpec(memory_space=pl.ANY))
        args.append(h_buf)
    return pl.pallas_call(
        functools.partial(_outproj_body, aliased),
        grid=(t // tm,),
        in_specs=in_specs,
        out_specs=[
            pl.BlockSpec((tm, D_MODEL), lambda i: (i, 0)),
            pl.BlockSpec((tm * ROW_SUB, LANES), lambda i: (tok0 // tm + i, 0)),
            pl.BlockSpec((TOP_K, tm), lambda i: (0, i)),
            pl.BlockSpec((TOP_K, tm), lambda i: (0, i)),
            pl.BlockSpec((tm, LANES), lambda i: (i, 0)),
            pl.BlockSpec((N_EXPERTS, 1), lambda i: (0, 0)),
        ],
        out_shape=[
            jax.ShapeDtypeStruct((t, D_MODEL), F32),
            jax.ShapeDtypeStruct((T_ALL * ROW_SUB, LANES), F32),
            jax.ShapeDtypeStruct((TOP_K, t), jnp.int32),
            jax.ShapeDtypeStruct((TOP_K, t), jnp.int32),
            jax.ShapeDtypeStruct((t, LANES), F32),
            jax.ShapeDtypeStruct((N_EXPERTS, 1), F32),
        ],
        scratch_shapes=[pltpu.VMEM((N_EXPERTS, 1), F32)],
        input_output_aliases={len(args) - 1: 1} if aliased else {},
        compiler_params=_cp(("arbitrary",)),
        name=name,
    )(*args)


MOE_NBLK = 8


def _moe_body(te_ref, na_ref, tok_ref, tokn_ref, dstp_ref, h_hbm, wgu_ref, bgu_ref, wd_ref, bd_ref,
              perm_ref, y_hbm, xbuf, ybuf, wgu_s, wd_s, act_s, gsem, ssem):
    i = pl.program_id(0)
    last = MOE_TILES - 1
    n_active = na_ref[0]
    slot = i % 2
    oslot = 1 - slot

    unroll = 16

    def issue_rows(start_row):
        def body(j, carry):
            for u in range(unroll):
                start_row(j * unroll + u)
            return carry
        lax.fori_loop(0, MOE_TM // unroll, body, 0)

    def hbm_row(ref, idx):
        return ref.at[pl.ds(pl.multiple_of(idx, ROW_SUB), ROW_SUB)]

    def vmem_row(buf, sl, r):
        return buf.at[sl, pl.ds(pl.multiple_of(r * ROW_SUB, ROW_SUB), ROW_SUB)]

    def gather_row(rows_ref, sl, r):
        pltpu.make_async_copy(hbm_row(h_hbm, rows_ref[0, r]), vmem_row(xbuf, sl, r), gsem.at[sl]).start(priority=0)

    def scatter_row(r):
        pltpu.make_async_copy(vmem_row(ybuf, oslot, r), hbm_row(y_hbm, dstp_ref[0, r]),
                              ssem.at[oslot]).start(priority=1)

    def gather_rows(rows_ref, sl):
        issue_rows(lambda r: gather_row(rows_ref, sl, r))

    def gather_and_scatter_rows():
        def both(r):
            gather_row(tokn_ref, oslot, r)
            scatter_row(r)
        issue_rows(both)

    def wait_rows(buf, sem, sl):
        pltpu.make_async_copy(buf.at[sl], buf.at[sl], sem.at[sl]).wait()

    @pl.when(i == 0)
    def _():
        gather_rows(tok_ref, 0)
        ybuf[...] = jnp.zeros_like(ybuf)
        pad = (T_PAD - T_ALL) * ROW_SUB
        fills = [pltpu.make_async_copy(ybuf.at[0, pl.ds(0, pad)],
                                       y_hbm.at[pl.ds((k * T_PAD + T_ALL) * ROW_SUB, pad)], ssem.at[0])
                 for k in range(TOP_K)]
        fills += [pltpu.make_async_copy(ybuf.at[0],
                                        y_hbm.at[pl.ds((TOP_K * T_PAD + d * MOE_TM) * ROW_SUB, MOE_TM * ROW_SUB)],
                                        ssem.at[0]) for d in range(2)]
        for f in fills:
            f.start()
        for f in fills:
            f.wait()

    @pl.when(i < last)
    def _():
        gather_and_scatter_rows()

    @pl.when(i == last)
    def _():
        issue_rows(scatter_row)

    wait_rows(xbuf, gsem, slot)

    @pl.when(i < n_active)
    def _():
        e_prev = te_ref[jnp.maximum(i - 1, 0)]

        @pl.when(jnp.logical_or(i == 0, e_prev != te_ref[i]))
        def _():
            for b in range(MOE_NBLK):
                cs = slice(b * 256, (b + 1) * 256)
                wgu_s[:, cs] = jnp.dot(wgu_ref[:, cs].astype(BF16), perm_ref[...],
                                       preferred_element_type=F32).astype(BF16)
            wd_s[...] = wd_ref[...].astype(BF16)

        x = _load_rows(xbuf.at[slot]).astype(BF16)
        for b in range(MOE_NBLK):
            cs = slice(b * 256, (b + 1) * 256)
            gu = jnp.dot(x, wgu_s[:, cs], preferred_element_type=F32) + bgu_ref[:, cs]
            gate = jnp.minimum(gu[:, :128], SWIGLU_LIMIT)
            up = jnp.clip(gu[:, 128:], -SWIGLU_LIMIT, SWIGLU_LIMIT)
            act_s[:, b * 128:(b + 1) * 128] = ((up + 1.0) * (gate * _sigmoid(SWIGLU_ALPHA * gate))).astype(BF16)
        y = jnp.dot(act_s[...], wd_s[...], preferred_element_type=F32) + bd_ref[...]

        @pl.when(i >= 1)
        def _():
            wait_rows(ybuf, ssem, slot)

        _store_rows(ybuf.at[slot], y)

    @pl.when(i >= n_active)
    def _():
        wait_rows(ybuf, ssem, slot)

        @pl.when(i == last)
        def _():
            wait_rows(ybuf, ssem, oslot)


def _moe_call(layer, h_all, tile_expert, n_active, row_tok, row_dst, wgu, bgu_perm, wd, bd, perm):
    smem_rows = lambda f: pl.BlockSpec((None, 1, MOE_TM), f, memory_space=pltpu.SMEM)
    grid_spec = pltpu.PrefetchScalarGridSpec(
        num_scalar_prefetch=2,
        grid=(MOE_TILES,),
        in_specs=[
            smem_rows(lambda i, te, na: (i, 0, 0)),
            smem_rows(lambda i, te, na: (jnp.minimum(i + 1, MOE_TILES - 1), 0, 0)),
            smem_rows(lambda i, te, na: (jnp.where(i == 0, MOE_TILES, i - 1), 0, 0)),
            pl.BlockSpec(memory_space=pl.ANY),
            pl.BlockSpec((None, None, D_MODEL, 2 * D_MODEL), lambda i, te, na: (layer, te[i], 0, 0)),
            pl.BlockSpec((None, None, 1, 2 * D_MODEL), lambda i, te, na: (layer, te[i], 0, 0)),
            pl.BlockSpec((None, None, D_MODEL, D_MODEL), lambda i, te, na: (layer, te[i], 0, 0)),
            pl.BlockSpec((None, None, 1, D_MODEL), lambda i, te, na: (layer, te[i], 0, 0)),
            pl.BlockSpec((256, 256), lambda i, te, na: (0, 0)),
        ],
        out_specs=pl.BlockSpec(memory_space=pl.ANY),
        scratch_shapes=[
            pltpu.VMEM((2, MOE_TM * ROW_SUB, LANES), F32),
            pltpu.VMEM((2, MOE_TM * ROW_SUB, LANES), F32),
            pltpu.VMEM((D_MODEL, 2 * D_MODEL), BF16),
            pltpu.VMEM((D_MODEL, D_MODEL), BF16),
            pltpu.VMEM((MOE_TM, D_MODEL), BF16),
            pltpu.SemaphoreType.DMA((2,)),
            pltpu.SemaphoreType.DMA((2,)),
        ],
    )
    return pl.pallas_call(
        _moe_body,
        grid_spec=grid_spec,
        out_shape=jax.ShapeDtypeStruct((Y_ROWS * ROW_SUB, LANES), F32),
        compiler_params=_cp(("arbitrary",), 56 * 1024 * 1024),
        name="moe_experts",
    )(tile_expert, n_active, row_tok, row_tok, row_dst, h_all, wgu, bgu_perm, wd, bd, perm)


INV_GROUP = 16


def _inv_body(gs_ref, cn_ref, na_ref, dest_ref, src_hbm, src_s, sem):
    k = pl.program_id(0)

    def fill(lo, hi):
        def body(p, carry):
            src_s[p] = -1
            return carry
        lax.fori_loop(lo, hi, body, 0)

    @pl.when(k == 0)
    def _():
        end = na_ref[0] * MOE_TM
        for e in range(N_EXPERTS):
            fill(gs_ref[e] + cn_ref[e], gs_ref[e + 1] if e + 1 < N_EXPERTS else end)
        fill(end, P_PAD)

    def body(row, carry):
        val = k * T_ALL + row * LANES
        for c in range(0, LANES, INV_GROUP):
            dests = [dest_ref[row, c + u] for u in range(INV_GROUP)]
            for u in range(INV_GROUP):
                src_s[dests[u]] = val + (c + u)
        return carry
    lax.fori_loop(0, T_ALL // LANES, body, 0)

    @pl.when(k == TOP_K - 1)
    def _():
        cp = pltpu.make_async_copy(src_s, src_hbm, sem)
        cp.start()
        cp.wait()


def _inv_call(gstart, counts, n_active, dest):
    grid_spec = pltpu.PrefetchScalarGridSpec(
        num_scalar_prefetch=3,
        grid=(TOP_K,),
        in_specs=[pl.BlockSpec((None, T_ALL // LANES, LANES), lambda k, *_: (k, 0, 0), memory_space=pltpu.SMEM)],
        out_specs=pl.BlockSpec(memory_space=pl.ANY),
        scratch_shapes=[pltpu.SMEM((P_PAD,), jnp.int32), pltpu.SemaphoreType.DMA(())],
    )
    return pl.pallas_call(
        _inv_body,
        grid_spec=grid_spec,
        out_shape=jax.ShapeDtypeStruct((P_PAD,), jnp.int32),
        compiler_params=_cp(("arbitrary",)),
        name="route_inverse",
    )(gstart, counts, n_active, dest.reshape(TOP_K, T_ALL // LANES, LANES))


def _route(idx_t, rank_t, counts_f):
    counts = counts_f.reshape(N_EXPERTS).astype(jnp.int32)
    tiles = (counts + MOE_TM - 1) // MOE_TM
    tile_end = jnp.cumsum(tiles)
    gstart = (tile_end - tiles) * MOE_TM
    n_active = tile_end[N_EXPERTS - 1]
    ti = jnp.arange(MOE_TILES, dtype=jnp.int32)
    te = jnp.sum(tile_end[None, :] <= jnp.minimum(ti, n_active - 1)[:, None], axis=1, dtype=jnp.int32)
    te = jnp.minimum(te, N_EXPERTS - 1)
    eids = jnp.arange(N_EXPERTS, dtype=jnp.int32)
    dest = rank_t + jnp.sum(jnp.where(idx_t[:, :, None] == eids, gstart, 0), axis=-1, dtype=jnp.int32)
    n_active = n_active.reshape(1)
    src = _inv_call(gstart, counts, n_active, dest).reshape(P_PAD)
    valid = src >= 0
    k = src // T_ALL
    t = src - k * T_ALL
    p = jnp.arange(P_PAD, dtype=jnp.int32)
    dump = TOP_K * T_PAD + ((p // MOE_TM) % 2) * MOE_TM + p % MOE_TM
    row_tok = (jnp.where(valid, t, 0) * ROW_SUB).reshape(MOE_TILES, 1, MOE_TM)
    row_dst = (jnp.where(valid, k * T_PAD + t, dump) * ROW_SUB).reshape(MOE_TILES, 1, MOE_TM)
    first = ((TOP_K * T_PAD + MOE_TM + jnp.arange(MOE_TM, dtype=jnp.int32)) * ROW_SUB).reshape(1, 1, MOE_TM)
    return te, n_active, row_tok, jnp.concatenate([row_dst, first], axis=0)


def _weighted(p_ref, ys):
    acc = p_ref[:, 0:1] * _load_rows(ys[0])
    for k in range(1, TOP_K):
        acc = acc + p_ref[:, k:k + 1] * _load_rows(ys[k])
    return acc


def _combine_body(x_ref, g_ref, p_ref, y0, y1, y2, y3, o_ref):
    o_ref[...] = x_ref[...] + g_ref[...] * _weighted(p_ref, (y0, y1, y2, y3))


def _combine_norm_body(x_ref, g_ref, p_ref, y0, y1, y2, y3, nw_ref, o_ref):
    x = x_ref[...] + g_ref[...] * _weighted(p_ref, (y0, y1, y2, y3))
    o_ref[...] = x * lax.rsqrt(jnp.mean(x * x, axis=-1, keepdims=True) + EPS) * nw_ref[...]


def _combine(x, mod, layer, prr, y_all, tok0, tm, final_w=None, name="combine"):
    t = x.shape[0]
    rows = mod.shape[2]
    tps = (t // mod.shape[1]) // tm
    yspec = lambda k: pl.BlockSpec((tm * ROW_SUB, LANES), lambda i: ((k * T_PAD + tok0) // tm + i, 0))
    in_specs = [pl.BlockSpec((tm, D_MODEL), lambda i: (i, 0)), _mod_spec(rows, layer, 5, tps),
                pl.BlockSpec((tm, LANES), lambda i: (i, 0)), yspec(0), yspec(1), yspec(2), yspec(3)]
    args = [x, mod, prr, y_all, y_all, y_all, y_all]
    body = _combine_body
    if final_w is not None:
        in_specs.append(_resident((1, D_MODEL)))
        args.append(final_w)
        body = _combine_norm_body
    return pl.pallas_call(
        body,
        grid=(t // tm,),
        in_specs=in_specs,
        out_specs=pl.BlockSpec((tm, D_MODEL), lambda i: (i, 0)),
        out_shape=jax.ShapeDtypeStruct((t, D_MODEL), F32),
        compiler_params=_cp(("arbitrary",)),
        name=name,
    )(*args)


def _mixer_out_and_moe(layer, x_p, a_p, x_s, a_s, w_out, mod_p, mod_s, nw, rwt, rb, tm_p, tm_s,
                       moe_w_gate_up, bgu_perm, moe_w_down, moe_b_down, perm):
    zero_cnt = jnp.zeros((N_EXPERTS, 1), F32)
    x1_p, h_all, idx_p, rank_p, prr_p, cnt = _outproj(x_p, a_p, w_out, mod_p, layer, nw, rwt, rb, tm_p, zero_cnt,
                                                      None, "outproj%d_prompt" % layer)
    x1_s, h_all, idx_s, rank_s, prr_s, cnt = _outproj(x_s, a_s, w_out, mod_s, layer, nw, rwt, rb, tm_s, cnt,
                                                      h_all, "outproj%d_sample" % layer)
    idx_t = jnp.concatenate([idx_p, idx_s], axis=1)
    rank_t = jnp.concatenate([rank_p, rank_s], axis=1)
    te, n_active, row_tok, row_dst = _route(idx_t, rank_t, cnt)
    y_all = _moe_call(layer, h_all, te, n_active, row_tok, row_dst, moe_w_gate_up, bgu_perm,
                      moe_w_down, moe_b_down.reshape(-1, N_EXPERTS, 1, D_MODEL), perm)
    return x1_p, x1_s, prr_p, prr_s, y_all


def kernel(x_prompt, x_sample, c_prompt, c_sample, state_ssm, state_ssm_conv, state_sconv, ada_w, ada_b, norm1_w, norm2_w, ssd_w_in, ssd_conv_w, ssd_conv_b, ssd_dt_bias, ssd_A_log, ssd_D, ssd_norm_w, ssd_w_out, sc_w_in, sc_conv_w, sc_w_out, router_w, router_b, moe_w_gate_up, moe_b_gate_up, moe_w_down, moe_b_down, final_norm_w):
    tm_p = 256
    tm_s = DEC_BATCH
    depth = ada_w.shape[0]

    def hilo(w):
        hi = w.astype(BF16)
        return jnp.stack([hi, (w - hi.astype(F32)).astype(BF16)])

    w_in0 = hilo(jnp.concatenate(
        [ssd_w_in[:, SSD_INNER:SSD_INNER + SSD_CONV_DIM], ssd_w_in[:, :SSD_INNER],
         ssd_w_in[:, SSD_INNER + SSD_CONV_DIM:], jnp.zeros((D_MODEL, HEADS_PAD - SSD_HEADS), F32)],
        axis=1))
    w_out0 = hilo(ssd_w_out)
    w_in1 = sc_w_in.astype(BF16)
    w_out1 = sc_w_out.astype(BF16)
    pad_h = lambda v: jnp.concatenate([v, jnp.zeros((HEADS_PAD - SSD_HEADS,), F32)]).reshape(1, HEADS_PAD)
    dtb = pad_h(ssd_dt_bias)
    alog = pad_h(ssd_A_log)
    dch = jnp.repeat(ssd_D, SSD_HEAD_DIM).reshape(1, SSD_INNER)
    ssd_nw = ssd_norm_w.reshape(1, SSD_INNER)
    conv_b = ssd_conv_b.reshape(1, SSD_CONV_DIM)
    emat = (jnp.arange(SSD_INNER, dtype=jnp.int32)[None, :] // SSD_HEAD_DIM
            == jnp.arange(HEADS_PAD, dtype=jnp.int32)[:, None]).astype(BF16)
    jj = jnp.arange(256, dtype=jnp.int32)
    perm = (jj[None, :] == jnp.where(jj % 2 == 0, jj // 2, 128 + jj // 2)[:, None]).astype(BF16)
    bgu_perm = moe_b_gate_up.reshape(depth, N_EXPERTS, 8, 128, 2).transpose(0, 1, 2, 4, 3).reshape(
        depth, N_EXPERTS, 1, 2 * D_MODEL)
    rwt = jnp.transpose(router_w, (0, 2, 1))
    rb = router_b.reshape(depth, N_EXPERTS, 1)
    n1 = norm1_w.reshape(depth, 1, D_MODEL)
    n2 = norm2_w.reshape(depth, 1, D_MODEL)

    mod = _ada_call(jnp.concatenate([c_prompt, c_sample], axis=0), ada_w, ada_b)
    mod_p = mod[:, :BATCH].reshape(depth, BATCH, 1, 6 * D_MODEL)
    mod_s = mod[:, BATCH:].reshape(depth, 1, DEC_BATCH, 6 * D_MODEL)

    xp = x_prompt.reshape(T_PROMPT, D_MODEL)
    xs = x_sample.reshape(DEC_BATCH, D_MODEL)

    zx_p = _normmod_matmul(xp, n1[0], mod_p, 0, 1, 0, w_in0, tm_p, "inproj0_prompt")
    gn_p, ssm_p, ssm_conv_p = _ssd_prompt(zx_p, ssd_conv_w, conv_b, dtb, alog, dch, ssd_nw, emat)
    zx_s = _normmod_matmul(xs, n1[0], mod_s, 0, 1, 0, w_in0, tm_s, "inproj0_sample")
    xc_s, xdt_s, dec_s, ncs_s = _ssd_sample_prep(
        zx_s, state_ssm_conv.reshape(DEC_BATCH, (SSD_CONV - 1) * SSD_CONV_DIM), ssd_conv_w, conv_b, dtb, alog, emat)
    steps = DEC_BATCH // SAMPLE_BS
    to_cols = lambda a: a.reshape(steps, SAMPLE_BS, SSD_INNER).transpose(0, 2, 1)
    b3 = xc_s[:, SSD_INNER:SSD_INNER + SSD_GN].reshape(DEC_BATCH, SSD_GROUPS, SSD_STATE)
    c3 = xc_s[:, SSD_INNER + SSD_GN:].reshape(DEC_BATCH, SSD_GROUPS, SSD_STATE).transpose(1, 0, 2)
    ssm_s, gn_s = _ssd_sample_state(state_ssm, to_cols(xdt_s), dec_s, b3, c3, xc_s, zx_s, dch, ssd_nw)
    ssm_conv_s = ncs_s.reshape(DEC_BATCH, SSD_CONV - 1, SSD_CONV_DIM)

    x1_p, x1_s, prr_p, prr_s, y0 = _mixer_out_and_moe(
        0, xp, gn_p, xs, gn_s, w_out0, mod_p, mod_s, n2[0], rwt[0], rb[0], tm_p, tm_s,
        moe_w_gate_up, bgu_perm, moe_w_down, moe_b_down, perm)
    x2_p = _combine(x1_p, mod_p, 0, prr_p, y0, 0, tm_p, name="combine0_prompt")
    x2_s = _combine(x1_s, mod_s, 0, prr_s, y0, T_PROMPT, tm_s, name="combine0_sample")

    bcx_p = _normmod_matmul(x2_p, n1[1], mod_p, 1, 1, 0, w_in1, tm_p, "inproj1_prompt")
    v_p, sconv_p = _sconv_prompt(bcx_p, sc_conv_w)
    bcx_s = _normmod_matmul(x2_s, n1[1], mod_s, 1, 1, 0, w_in1, tm_s, "inproj1_sample")
    v_s, nsc_s = _sconv_sample(bcx_s, state_sconv.reshape(DEC_BATCH, (SC_WIDTH - 1) * D_MODEL), sc_conv_w)
    sconv_s = nsc_s.reshape(DEC_BATCH, SC_WIDTH - 1, D_MODEL)

    x3_p, x3_s, prr_p, prr_s, y1 = _mixer_out_and_moe(
        1, x2_p, v_p, x2_s, v_s, w_out1, mod_p, mod_s, n2[1], rwt[1], rb[1], tm_p, tm_s,
        moe_w_gate_up, bgu_perm, moe_w_down, moe_b_down, perm)
    fw = final_norm_w.reshape(1, D_MODEL)
    y_p = _combine(x3_p, mod_p, 1, prr_p, y1, 0, tm_p, final_w=fw, name="final_prompt")
    y_s = _combine(x3_s, mod_s, 1, prr_s, y1, T_PROMPT, tm_s, final_w=fw, name="final_sample")

    return (y_p.reshape(BATCH, SEQ, D_MODEL), y_s.reshape(DEC_BATCH, 1, D_MODEL), ssm_p, ssm_conv_p, sconv_p,
            ssm_s, ssm_conv_s, sconv_s)
```

```python
import functools

import jax
import jax.numpy as jnp
from jax import lax
from jax.experimental import pallas as pl
from jax.experimental.pallas import tpu as pltpu

F32 = jnp.float32
BF16 = jnp.bfloat16

D_MODEL = 1024
BATCH = 8
SEQ = 2048
DEC_BATCH = 128
SSD_INNER = 2048
SSD_HEAD_DIM = 64
SSD_HEADS = 32
SSD_GROUPS = 8
SSD_HPG = 4
SSD_STATE = 128
SSD_CONV = 4
SSD_CHUNK = 128
SSD_GN = 1024
SSD_CONV_DIM = 4096
SC_WIDTH = 3
N_EXPERTS = 32
TOP_K = 4
SWIGLU_LIMIT = 7.0
SWIGLU_ALPHA = 1.702
EPS = 1e-5

LANES = 128
HEADS_PAD = LANES
ZX_COLS = SSD_CONV_DIM + SSD_INNER + HEADS_PAD
T_PROMPT = BATCH * SEQ
T_ALL = T_PROMPT + DEC_BATCH
MOE_TM = 256
T_PAD = 16640
N_ASSIGN = TOP_K * T_ALL
MOE_TILES = (N_ASSIGN + N_EXPERTS * (MOE_TM - 1)) // MOE_TM + 1
P_PAD = MOE_TILES * MOE_TM
Y_ROWS = TOP_K * T_PAD + 2 * MOE_TM
VMEM_LIMIT = 48 * 1024 * 1024


def _cp(sem, vmem=VMEM_LIMIT):
    return pltpu.CompilerParams(dimension_semantics=sem, vmem_limit_bytes=vmem)


def _sigmoid(x):
    return 1.0 / (1.0 + jnp.exp(-x))


def _silu(x):
    return x * _sigmoid(x)


def _softplus(x):
    return jnp.maximum(x, 0.0) + jnp.log1p(jnp.exp(-jnp.abs(x)))


def _normmod(x, nw, sc, sh):
    y = x * lax.rsqrt(jnp.mean(x * x, axis=-1, keepdims=True) + EPS)
    return y * nw * (1.0 + sc) + sh


def _split3(x):
    hi = x.astype(BF16)
    r1 = x - hi.astype(F32)
    mid = r1.astype(BF16)
    lo = (r1 - mid.astype(F32)).astype(BF16)
    return hi, mid, lo


def _dot_exact_lhs(x, m01):
    hi, mid, lo = _split3(x)
    d = functools.partial(jnp.dot, preferred_element_type=F32)
    return d(hi, m01) + d(mid, m01) + d(lo, m01)


def _dot_exact_rhs(m01, x):
    hi, mid, lo = _split3(x)
    d = functools.partial(jnp.dot, preferred_element_type=F32)
    return d(m01, hi) + d(m01, mid) + d(m01, lo)


def _dot_nt(a, b):
    return lax.dot_general(a, b, (((1,), (1,)), ((), ())), preferred_element_type=F32)


def _split2(x):
    hi = x.astype(BF16)
    return hi, (x - hi.astype(F32)).astype(BF16)


def _dot_x3(a, b, nt=False):
    d = _dot_nt if nt else functools.partial(jnp.dot, preferred_element_type=F32)
    return d(a[0], b[0]) + d(a[1], b[0]) + d(a[0], b[1])


ROW_SUB = D_MODEL // LANES


def _load_rows(ref):
    tm = ref.shape[0] // ROW_SUB
    return jnp.concatenate([ref[pl.ds(c, tm, stride=ROW_SUB), :] for c in range(ROW_SUB)], axis=1)


def _store_rows(ref, x):
    tm = ref.shape[0] // ROW_SUB
    for c in range(ROW_SUB):
        ref[pl.ds(c, tm, stride=ROW_SUB), :] = x[:, c * LANES:(c + 1) * LANES]


def _ada_body(c_ref, w_ref, b_ref, o_ref):
    o_ref[...] = _dot_x3(_split2(_silu(c_ref[...])), _split2(w_ref[...])) + b_ref[...]


def _ada_call(c_all, ada_w, ada_b):
    n = c_all.shape[0]
    depth = ada_w.shape[0]
    return pl.pallas_call(
        _ada_body,
        grid=(depth, 6),
        in_specs=[
            pl.BlockSpec((n, D_MODEL), lambda l, j: (0, 0)),
            pl.BlockSpec((None, D_MODEL, D_MODEL), lambda l, j: (l, 0, j)),
            pl.BlockSpec((None, 1, D_MODEL), lambda l, j: (l, 0, j)),
        ],
        out_specs=pl.BlockSpec((None, n, D_MODEL), lambda l, j: (l, 0, j)),
        out_shape=jax.ShapeDtypeStruct((depth, n, 6 * D_MODEL), F32),
        compiler_params=_cp(("arbitrary", "arbitrary")),
        name="ada_mod",
    )(c_all, ada_w, ada_b.reshape(depth, 1, 6 * D_MODEL))


def _col_chunks(n, step=512):
    return [(s, min(step, n - s)) for s in range(0, n, step)]


def _rows_x3(a_hi, a_hilo, w_ref, s, n):
    tm = a_hi.shape[0]
    r = jnp.dot(a_hilo, w_ref[0, :, s:s + n], preferred_element_type=F32)
    return r[:tm] + r[tm:] + jnp.dot(a_hi, w_ref[1, :, s:s + n], preferred_element_type=F32)


def _nm_body(x_ref, nw_ref, sc_ref, sh_ref, w_ref, o_ref):
    h = _normmod(x_ref[...], nw_ref[...], sc_ref[...], sh_ref[...])
    if len(w_ref.shape) == 3:
        h_hi, h_lo = _split2(h)
        h_hilo = jnp.concatenate([h_hi, h_lo], axis=0)
        for s, n in _col_chunks(o_ref.shape[1]):
            o_ref[:, s:s + n] = _rows_x3(h_hi, h_hilo, w_ref, s, n)
    else:
        h = h.astype(BF16)
        for s, n in _col_chunks(o_ref.shape[1]):
            o_ref[:, s:s + n] = jnp.dot(h, w_ref[:, s:s + n], preferred_element_type=F32)


def _mod_spec(rows, layer, col, tiles_per_seq):
    return pl.BlockSpec((None, None, rows, D_MODEL), lambda i: (layer, i // tiles_per_seq, 0, col))


def _resident(shape):
    return pl.BlockSpec(shape, lambda *_: tuple(0 for _ in shape), pipeline_mode=pl.Buffered(1))


def _normmod_matmul(x, nw, mod, layer, sc_col, sh_col, w_bf, tm, name):
    t = x.shape[0]
    n = w_bf.shape[-1]
    rows = mod.shape[2]
    tps = (t // mod.shape[1]) // tm
    return pl.pallas_call(
        _nm_body,
        grid=(t // tm,),
        in_specs=[
            pl.BlockSpec((tm, D_MODEL), lambda i: (i, 0)),
            _resident((1, D_MODEL)),
            _mod_spec(rows, layer, sc_col, tps),
            _mod_spec(rows, layer, sh_col, tps),
            _resident(w_bf.shape),
        ],
        out_specs=pl.BlockSpec((tm, n), lambda i: (i, 0)),
        out_shape=jax.ShapeDtypeStruct((t, n), F32),
        compiler_params=_cp(("arbitrary",)),
        name=name,
    )(x, nw, mod, mod, w_bf)


def _ssd_prompt_body(xbc_ref, z_ref, dt_ref, cw_ref, cb_ref, dtb_ref, alog_ref, dch_ref, nw_ref, e_ref,
                     gn_ref, st_ref, cs_ref, cbuf, ht, xc_s, yd_s):
    q = SSD_CHUNK
    c = pl.program_id(1)

    @pl.when(c == 0)
    def _():
        cbuf[0:8, :] = jnp.zeros((8, SSD_CONV_DIM), F32)
        ht[...] = jnp.zeros_like(ht)

    cbuf[8:8 + q, :] = xbc_ref[...]
    for s, n in _col_chunks(SSD_CONV_DIM):
        acc = cbuf[5:5 + q, s:s + n] * cw_ref[0:1, s:s + n]
        for k in range(1, SSD_CONV):
            acc = acc + cbuf[5 + k:5 + k + q, s:s + n] * cw_ref[k:k + 1, s:s + n]
        xc_s[:, s:s + n] = _silu(acc + cb_ref[:, s:s + n])
    cs_ref[...] = cbuf[5 + q:8 + q, :]
    cbuf[0:8, :] = cbuf[q:q + 8, :]

    dt = _softplus(dt_ref[...] + dtb_ref[...])
    da = dt * (-jnp.exp(alog_ref[...]))
    ri = lax.broadcasted_iota(jnp.int32, (q, q), 0)
    ci = lax.broadcasted_iota(jnp.int32, (q, q), 1)
    causal = ri >= ci
    tri = jnp.where(causal, 1.0, 0.0).astype(BF16)
    tri_t = jnp.where(ri <= ci, 1.0, 0.0).astype(BF16)
    acs = _dot_exact_rhs(tri, da)
    acs_t = _dot_exact_lhs(da.T, tri_t)

    for g in range(SSD_GROUPS):
        cs = slice(g * 256, (g + 1) * 256)
        e_g = e_ref[:, cs]
        dte = _dot_exact_lhs(dt, e_g)
        ace = _dot_exact_lhs(acs, e_g)
        last = ace[q - 1:q, :]
        xs_g = xc_s[:, cs]
        xdt = xs_g * dte
        b_g = xc_s[:, SSD_INNER + g * SSD_STATE:SSD_INNER + (g + 1) * SSD_STATE]
        c_g = xc_s[:, SSD_INNER + SSD_GN + g * SSD_STATE:SSD_INNER + SSD_GN + (g + 1) * SSD_STATE]
        c_2 = _split2(c_g)
        cb = _dot_x3(c_2, _split2(b_g), nt=True)
        h_prev = ht[g]
        y_off = _dot_x3(c_2, _split2(h_prev)) * jnp.exp(ace)
        st_t = _dot_x3(_split2(b_g.T), _split2(jnp.exp(last - ace) * xdt))
        ht[g] = jnp.exp(last) * h_prev + st_t
        xdt_hi, xdt_lo = _split2(xdt)
        for r in range(SSD_HPG):
            h = g * SSD_HPG + r
            hs = slice(r * 64, (r + 1) * 64)
            diff = acs[:, h:h + 1] - acs_t[h:h + 1, :]
            m_2 = _split2(cb * jnp.where(causal, jnp.exp(diff), 0.0))
            yd_s[:, hs] = _dot_x3(m_2, (xdt_hi[:, hs], xdt_lo[:, hs]))
        y = yd_s[...] + y_off + xs_g * dch_ref[:, cs]
        gg = y * _silu(z_ref[:, cs])
        ms = jnp.mean(gg * gg, axis=-1, keepdims=True)
        gn_ref[:, cs] = gg * lax.rsqrt(ms + EPS) * nw_ref[:, cs]

    @pl.when(c == pl.num_programs(1) - 1)
    def _():
        for g in range(SSD_GROUPS):
            t = ht[g].T
            for r in range(SSD_HPG):
                st_ref[g * SSD_HPG + r] = t[r * 64:(r + 1) * 64, :]


def _ssd_prompt(zx, conv_w, conv_b, dtb, alog, dch, nw, emat):
    nc = SEQ // SSD_CHUNK
    q = SSD_CHUNK
    row = lambda b, c: b * nc + c
    return pl.pallas_call(
        _ssd_prompt_body,
        grid=(BATCH, nc),
        in_specs=[
            pl.BlockSpec((q, SSD_CONV_DIM), lambda b, c: (row(b, c), 0)),
            pl.BlockSpec((q, SSD_INNER), lambda b, c: (row(b, c), 2)),
            pl.BlockSpec((q, HEADS_PAD), lambda b, c: (row(b, c), (SSD_CONV_DIM + SSD_INNER) // HEADS_PAD)),
            _resident((SSD_CONV, SSD_CONV_DIM)),
            _resident((1, SSD_CONV_DIM)),
            _resident((1, HEADS_PAD)),
            _resident((1, HEADS_PAD)),
            _resident((1, SSD_INNER)),
            _resident((1, SSD_INNER)),
            _resident((HEADS_PAD, SSD_INNER)),
        ],
        out_specs=[
            pl.BlockSpec((q, SSD_INNER), lambda b, c: (row(b, c), 0)),
            pl.BlockSpec((None, SSD_HEADS, SSD_HEAD_DIM, SSD_STATE), lambda b, c: (b, 0, 0, 0)),
            pl.BlockSpec((None, SSD_CONV - 1, SSD_CONV_DIM), lambda b, c: (b, 0, 0)),
        ],
        out_shape=[
            jax.ShapeDtypeStruct((T_PROMPT, SSD_INNER), F32),
            jax.ShapeDtypeStruct((BATCH, SSD_HEADS, SSD_HEAD_DIM, SSD_STATE), F32),
            jax.ShapeDtypeStruct((BATCH, SSD_CONV - 1, SSD_CONV_DIM), F32),
        ],
        scratch_shapes=[
            pltpu.VMEM((8 + q, SSD_CONV_DIM), F32),
            pltpu.VMEM((SSD_GROUPS, SSD_STATE, 256), F32),
            pltpu.VMEM((q, SSD_CONV_DIM), F32),
            pltpu.VMEM((q, 256), F32),
        ],
        compiler_params=_cp(("arbitrary", "arbitrary")),
        name="ssd_prompt",
    )(zx, zx, zx, conv_w, conv_b, dtb, alog, dch, nw, emat)


def _ssd_sample_prep_body(zx_ref, cst_ref, cw_ref, cb_ref, dtb_ref, alog_ref, e_ref,
                          xc_ref, xdt_ref, dec_ref, ncs_ref):
    kc = SSD_CONV_DIM
    for s, n in _col_chunks(kc):
        xnew = zx_ref[:, s:s + n]
        acc = xnew * cw_ref[SSD_CONV - 1:SSD_CONV, s:s + n]
        for k in range(SSD_CONV - 1):
            acc = acc + cst_ref[:, k * kc + s:k * kc + s + n] * cw_ref[k:k + 1, s:s + n]
        xc_ref[:, s:s + n] = _silu(acc + cb_ref[:, s:s + n])
        for k in range(SSD_CONV - 2):
            ncs_ref[:, k * kc + s:k * kc + s + n] = cst_ref[:, (k + 1) * kc + s:(k + 1) * kc + s + n]
        ncs_ref[:, (SSD_CONV - 2) * kc + s:(SSD_CONV - 2) * kc + s + n] = xnew
    dt = _softplus(zx_ref[:, SSD_CONV_DIM + SSD_INNER:] + dtb_ref[...])
    dec_ref[...] = jnp.exp(dt * (-jnp.exp(alog_ref[...])))
    for s, n in _col_chunks(SSD_INNER):
        xdt_ref[:, s:s + n] = xc_ref[:, s:s + n] * _dot_exact_lhs(dt, e_ref[:, s:s + n])


def _ssd_sample_prep(zx, cst, conv_w, conv_b, dtb, alog, emat):
    n = DEC_BATCH
    full = lambda shape: pl.BlockSpec(shape, lambda i: tuple(0 for _ in shape))
    return pl.pallas_call(
        _ssd_sample_prep_body,
        grid=(1,),
        in_specs=[full((n, ZX_COLS)), full((n, (SSD_CONV - 1) * SSD_CONV_DIM)), full((SSD_CONV, SSD_CONV_DIM)),
                  full((1, SSD_CONV_DIM)), full((1, HEADS_PAD)), full((1, HEADS_PAD)),
                  full((HEADS_PAD, SSD_INNER))],
        out_specs=[full((n, SSD_CONV_DIM)), full((n, SSD_INNER)), full((n, HEADS_PAD)),
                   full((n, (SSD_CONV - 1) * SSD_CONV_DIM))],
        out_shape=[
            jax.ShapeDtypeStruct((n, SSD_CONV_DIM), F32),
            jax.ShapeDtypeStruct((n, SSD_INNER), F32),
            jax.ShapeDtypeStruct((n, HEADS_PAD), F32),
            jax.ShapeDtypeStruct((n, (SSD_CONV - 1) * SSD_CONV_DIM), F32),
        ],
        compiler_params=_cp(("arbitrary",)),
        name="ssd_sample_prep",
    )(zx, cst, conv_w, conv_b, dtb, alog, emat)


SAMPLE_BS = 8


def _ssd_sample_state_body(dec_ref, st_ref, xt_ref, b_ref, c_ref, xs_ref, z_ref, dch_ref, nw_ref,
                           so_ref, gn_ref, y_s):
    def group(g, carry):
        r0 = pl.multiple_of(g * 256, 256)
        c_2 = _split2(c_ref[g])
        for i in range(SAMPLE_BS):
            h = st_ref[i, pl.ds(g * SSD_HPG, SSD_HPG)]
            hd = jnp.concatenate([h[r] * dec_ref[i, g * SSD_HPG + r] for r in range(SSD_HPG)], axis=0)
            xcol = xt_ref[pl.ds(r0, 256), i:i + 1]
            brow = b_ref[i, pl.ds(g, 1), :]
            hn = hd + xcol * brow
            so_ref[i, pl.ds(g * SSD_HPG, SSD_HPG)] = hn.reshape(SSD_HPG, SSD_HEAD_DIM, SSD_STATE)
            yall = _dot_x3(c_2, _split2(hn), nt=True)
            y_s[g, i:i + 1, :] = yall[i:i + 1, :]
        return carry
    lax.fori_loop(0, SSD_GROUPS, group, 0)
    for g in range(SSD_GROUPS):
        cs = slice(g * 256, (g + 1) * 256)
        y = y_s[g] + xs_ref[:, cs] * dch_ref[:, cs]
        gg = y * _silu(z_ref[:, cs])
        ms = jnp.mean(gg * gg, axis=-1, keepdims=True)
        gn_ref[:, cs] = gg * lax.rsqrt(ms + EPS) * nw_ref[:, cs]


def _ssd_sample_state(state, xt, dec, b3, c3, xc, zx, dch, nw):
    bs = SAMPLE_BS
    steps = DEC_BATCH // bs
    return pl.pallas_call(
        _ssd_sample_state_body,
        grid=(steps,),
        in_specs=[
            pl.BlockSpec((None, bs, SSD_HEADS), lambda s: (s, 0, 0), memory_space=pltpu.SMEM),
            pl.BlockSpec((bs, SSD_HEADS, SSD_HEAD_DIM, SSD_STATE), lambda s: (s, 0, 0, 0)),
            pl.BlockSpec((None, SSD_INNER, bs), lambda s: (s, 0, 0)),
            pl.BlockSpec((bs, SSD_GROUPS, SSD_STATE), lambda s: (s, 0, 0)),
            pl.BlockSpec((SSD_GROUPS, bs, SSD_STATE), lambda s: (0, s, 0)),
            pl.BlockSpec((bs, SSD_INNER), lambda s: (s, 0)),
            pl.BlockSpec((bs, SSD_INNER), lambda s: (s, 2)),
            _resident((1, SSD_INNER)),
            _resident((1, SSD_INNER)),
        ],
        out_specs=[
            pl.BlockSpec((bs, SSD_HEADS, SSD_HEAD_DIM, SSD_STATE), lambda s: (s, 0, 0, 0)),
            pl.BlockSpec((bs, SSD_INNER), lambda s: (s, 0)),
        ],
        out_shape=[
            jax.ShapeDtypeStruct(state.shape, F32),
            jax.ShapeDtypeStruct((DEC_BATCH, SSD_INNER), F32),
        ],
        scratch_shapes=[pltpu.VMEM((SSD_GROUPS, bs, 256), F32)],
        compiler_params=_cp(("arbitrary",), 56 * 1024 * 1024),
        name="ssd_sample_state",
    )(dec[:, :SSD_HEADS].reshape(steps, bs, SSD_HEADS), state, xt, b3, c3, xc, zx, dch, nw)


def _sconv_prompt_body(bg_ref, cg_ref, xh_ref, cw_ref, v_ref, st_ref, cbuf):
    tm = bg_ref.shape[0]
    j = pl.program_id(1)

    @pl.when(j == 0)
    def _():
        cbuf[0:8, :] = jnp.zeros((8, D_MODEL), F32)

    cbuf[8:8 + tm, :] = cg_ref[...] * xh_ref[...]
    acc = cbuf[6:6 + tm, :] * cw_ref[0:1, :]
    for k in range(1, SC_WIDTH):
        acc = acc + cbuf[6 + k:6 + k + tm, :] * cw_ref[k:k + 1, :]
    v_ref[...] = (bg_ref[...] * acc).astype(BF16)
    st_ref[...] = cbuf[6 + tm:8 + tm, :]
    cbuf[0:8, :] = cbuf[tm:tm + 8, :]


def _sconv_prompt(bcx, conv_w, tm=256):
    nt = SEQ // tm
    row = lambda b, j: b * nt + j
    return pl.pallas_call(
        _sconv_prompt_body,
        grid=(BATCH, nt),
        in_specs=[
            pl.BlockSpec((tm, D_MODEL), lambda b, j: (row(b, j), 0)),
            pl.BlockSpec((tm, D_MODEL), lambda b, j: (row(b, j), 1)),
            pl.BlockSpec((tm, D_MODEL), lambda b, j: (row(b, j), 2)),
            _resident((SC_WIDTH, D_MODEL)),
        ],
        out_specs=[
            pl.BlockSpec((tm, D_MODEL), lambda b, j: (row(b, j), 0)),
            pl.BlockSpec((None, SC_WIDTH - 1, D_MODEL), lambda b, j: (b, 0, 0)),
        ],
        out_shape=[
            jax.ShapeDtypeStruct((T_PROMPT, D_MODEL), BF16),
            jax.ShapeDtypeStruct((BATCH, SC_WIDTH - 1, D_MODEL), F32),
        ],
        scratch_shapes=[pltpu.VMEM((8 + tm, D_MODEL), F32)],
        compiler_params=_cp(("arbitrary", "arbitrary")),
        name="sconv_prompt",
    )(bcx, bcx, bcx, conv_w)


def _sconv_sample_body(bcx_ref, st_ref, cw_ref, v_ref, ns_ref):
    d = D_MODEL
    u = bcx_ref[:, d:2 * d] * bcx_ref[:, 2 * d:3 * d]
    acc = u * cw_ref[SC_WIDTH - 1:SC_WIDTH, :]
    for k in range(SC_WIDTH - 1):
        acc = acc + st_ref[:, k * d:(k + 1) * d] * cw_ref[k:k + 1, :]
    v_ref[...] = (bcx_ref[:, 0:d] * acc).astype(BF16)
    for k in range(SC_WIDTH - 2):
        ns_ref[:, k * d:(k + 1) * d] = st_ref[:, (k + 1) * d:(k + 2) * d]
    ns_ref[:, (SC_WIDTH - 2) * d:] = u


def _sconv_sample(bcx, st, conv_w):
    n = DEC_BATCH
    full = lambda shape: pl.BlockSpec(shape, lambda i: tuple(0 for _ in shape))
    return pl.pallas_call(
        _sconv_sample_body,
        grid=(1,),
        in_specs=[full((n, 3 * D_MODEL)), full((n, (SC_WIDTH - 1) * D_MODEL)), full((SC_WIDTH, D_MODEL))],
        out_specs=[full((n, D_MODEL)), full((n, (SC_WIDTH - 1) * D_MODEL))],
        out_shape=[jax.ShapeDtypeStruct((n, D_MODEL), BF16),
                   jax.ShapeDtypeStruct((n, (SC_WIDTH - 1) * D_MODEL), F32)],
        compiler_params=_cp(("arbitrary",)),
        name="sconv_sample",
    )(bcx, st, conv_w)


def _outproj_body(aliased, x_ref, a_ref, w_ref, g1_ref, nw_ref, sc_ref, sh_ref, rwt_ref, rb_ref, u_ref, cin_ref,
                  *refs):
    x1_ref, h2_ref, idx_ref, rank_ref, prr_ref, cnt_ref, run = refs[1:] if aliased else refs

    @pl.when(pl.program_id(0) == 0)
    def _():
        run[...] = cin_ref[...]

    if len(w_ref.shape) == 3:
        a_hi, a_lo = _split2(a_ref[...])
        m = _rows_x3(a_hi, jnp.concatenate([a_hi, a_lo], axis=0), w_ref, 0, D_MODEL)
    else:
        m = jnp.dot(a_ref[...], w_ref[...], preferred_element_type=F32)
    x1 = x_ref[...] + g1_ref[...] * m
    x1_ref[...] = x1
    h2 = _normmod(x1, nw_ref[...], sc_ref[...], sh_ref[...])
    _store_rows(h2_ref, h2)
    h_hi = h2.astype(BF16)
    h_lo = (h2 - h_hi.astype(F32)).astype(BF16)
    rw = rwt_ref[...]
    w_hi = rw.astype(BF16)
    w_lo = (rw - w_hi.astype(F32)).astype(BF16)
    logits = _dot_nt(w_hi, h_hi) + _dot_nt(w_hi, h_lo) + _dot_nt(w_lo, h_hi) + rb_ref[...]
    eio = lax.broadcasted_iota(jnp.int32, logits.shape, 0)
    vals, idxs = [], []
    for _ in range(TOP_K):
        mx = jnp.max(logits, axis=0, keepdims=True)
        sel = jnp.min(jnp.where(logits == mx, eio, N_EXPERTS), axis=0, keepdims=True)
        vals.append(mx)
        idxs.append(sel)
        logits = jnp.where(eio == sel, -jnp.inf, logits)
    ex = [jnp.exp(v - vals[0]) for v in vals]
    tot = ex[0] + ex[1] + ex[2] + ex[3]
    idx_ref[...] = jnp.concatenate(idxs, axis=0)
    tm = logits.shape[1]
    base = run[...]
    ranks = []
    for k in range(TOP_K):
        ohf = jnp.where(eio == idxs[k], 1.0, 0.0)
        pref = jnp.dot(ohf.astype(BF16), u_ref[...], preferred_element_type=F32)
        ranks.append(jnp.sum(ohf * (pref - 1.0 + base), axis=0, keepdims=True))
        base = base + pref[:, tm - 1:tm]
    run[...] = base
    cnt_ref[...] = base
    rank_ref[...] = jnp.concatenate(ranks, axis=0).astype(jnp.int32)
    pr = jnp.concatenate([e / tot for e in ex] + [jnp.zeros((LANES - TOP_K, tm), F32)], axis=0)
    prr_ref[...] = pr.T


def _outproj(x, a, w_bf, mod, layer, nw, rwt, rb, tm, cnt_in, h_buf, name):
    t = x.shape[0]
    k = a.shape[1]
    rows = mod.shape[2]
    tps = (t // mod.shape[1]) // tm
    aliased = h_buf is not None
    tok0 = T_PROMPT if aliased else 0
    umat = (jnp.arange(tm, dtype=jnp.int32)[:, None] <= jnp.arange(tm, dtype=jnp.int32)[None, :]).astype(BF16)
    in_specs = [
        pl.BlockSpec((tm, D_MODEL), lambda i: (i, 0)),
        pl.BlockSpec((tm, k), lambda i: (i, 0)),
        _resident(w_bf.shape),
        _mod_spec(rows, layer, 2, tps),
        _resident((1, D_MODEL)),
        _mod_spec(rows, layer, 4, tps),
        _mod_spec(rows, layer, 3, tps),
        _resident((N_EXPERTS, D_MODEL)),
        _resident((N_EXPERTS, 1)),
        _resident((tm, tm)),
        _resident((N_EXPERTS, 1)),
    ]
    args = [x, a, w_bf, mod, nw, mod, mod, rwt, rb, umat, cnt_in]
    if aliased:
        in_specs.append(pl.BlockSpec(memory_space=pl.ANY))
        args.append(h_buf)
    return pl.pallas_call(
        functools.partial(_outproj_body, aliased),
        grid=(t // tm,),
        in_specs=in_specs,
        out_specs=[
            pl.BlockSpec((tm, D_MODEL), lambda i: (i, 0)),
            pl.BlockSpec((tm * ROW_SUB, LANES), lambda i: (tok0 // tm + i, 0)),
            pl.BlockSpec((TOP_K, tm), lambda i: (0, i)),
            pl.BlockSpec((TOP_K, tm), lambda i: (0, i)),
            pl.BlockSpec((tm, LANES), lambda i: (i, 0)),
            pl.BlockSpec((N_EXPERTS, 1), lambda i: (0, 0)),
        ],
        out_shape=[
            jax.ShapeDtypeStruct((t, D_MODEL), F32),
            jax.ShapeDtypeStruct((T_ALL * ROW_SUB, LANES), F32),
            jax.ShapeDtypeStruct((TOP_K, t), jnp.int32),
            jax.ShapeDtypeStruct((TOP_K, t), jnp.int32),
            jax.ShapeDtypeStruct((t, LANES), F32),
            jax.ShapeDtypeStruct((N_EXPERTS, 1), F32),
        ],
        scratch_shapes=[pltpu.VMEM((N_EXPERTS, 1), F32)],
        input_output_aliases={len(args) - 1: 1} if aliased else {},
        compiler_params=_cp(("arbitrary",)),
        name=name,
    )(*args)


MOE_NBLK = 8


def _moe_body(te_ref, na_ref, tok_ref, tokn_ref, dstp_ref, h_hbm, wgu_ref, bgu_ref, wd_ref, bd_ref,
              perm_ref, y_hbm, xbuf, ybuf, wgu_s, wd_s, act_s, xb_s, gsem, ssem):
    i = pl.program_id(0)
    last = MOE_TILES - 1
    n_active = na_ref[0]
    slot = i % 2
    oslot = 1 - slot

    unroll = 16

    def issue_rows(start_row):
        def body(j, carry):
            for u in range(unroll):
                start_row(j * unroll + u)
            return carry
        lax.fori_loop(0, MOE_TM // unroll, body, 0)

    def hbm_row(ref, idx):
        return ref.at[pl.ds(pl.multiple_of(idx, ROW_SUB), ROW_SUB)]

    def vmem_row(buf, sl, r):
        return buf.at[sl, pl.ds(pl.multiple_of(r * ROW_SUB, ROW_SUB), ROW_SUB)]

    def gather_row(rows_ref, sl, r):
        pltpu.make_async_copy(hbm_row(h_hbm, rows_ref[0, r]), vmem_row(xbuf, sl, r), gsem.at[sl]).start(priority=0)

    def scatter_row(r):
        pltpu.make_async_copy(vmem_row(ybuf, oslot, r), hbm_row(y_hbm, dstp_ref[0, r]),
                              ssem.at[oslot]).start(priority=1)

    def gather_rows(rows_ref, sl):
        issue_rows(lambda r: gather_row(rows_ref, sl, r))

    def gather_and_scatter_rows():
        def both(r):
            gather_row(tokn_ref, oslot, r)
            scatter_row(r)
        issue_rows(both)

    def wait_rows(buf, sem, sl):
        pltpu.make_async_copy(buf.at[sl], buf.at[sl], sem.at[sl]).wait()

    @pl.when(i == 0)
    def _():
        gather_rows(tok_ref, 0)
        ybuf[...] = jnp.zeros_like(ybuf)
        pad = (T_PAD - T_ALL) * ROW_SUB
        fills = [pltpu.make_async_copy(ybuf.at[0, pl.ds(0, pad)],
                                       y_hbm.at[pl.ds((k * T_PAD + T_ALL) * ROW_SUB, pad)], ssem.at[0])
                 for k in range(TOP_K)]
        fills += [pltpu.make_async_copy(ybuf.at[0],
                                        y_hbm.at[pl.ds((TOP_K * T_PAD + d * MOE_TM) * ROW_SUB, MOE_TM * ROW_SUB)],
                                        ssem.at[0]) for d in range(2)]
        for f in fills:
            f.start()
        for f in fills:
            f.wait()

    wait_rows(xbuf, gsem, slot)

    @pl.when(i < n_active)
    def _():
        e_prev = te_ref[jnp.maximum(i - 1, 0)]

        @pl.when(jnp.logical_or(i == 0, e_prev != te_ref[i]))
        def _():
            for b in range(MOE_NBLK):
                wgu_s[b] = jnp.dot(wgu_ref[:, b * 256:(b + 1) * 256].astype(BF16), perm_ref[...],
                                   preferred_element_type=F32).astype(BF16)
            wd_s[...] = wd_ref[...].astype(BF16)

        xb_s[...] = _load_rows(xbuf.at[slot]).astype(BF16)
        per = MOE_TM // MOE_NBLK

        def block(b, carry):
            for u in range(per):
                gather_row(tokn_ref, oslot, b * per + u)
                scatter_row(b * per + u)
            gu = jnp.dot(xb_s[...], wgu_s[b], preferred_element_type=F32) + bgu_ref[pl.ds(b, 1), :]
            gate = jnp.minimum(gu[:, :128], SWIGLU_LIMIT)
            up = jnp.clip(gu[:, 128:], -SWIGLU_LIMIT, SWIGLU_LIMIT)
            act_s[b] = ((up + 1.0) * (gate * _sigmoid(SWIGLU_ALPHA * gate))).astype(BF16)
            return carry
        lax.fori_loop(0, MOE_NBLK, block, 0)
        act = jnp.concatenate([act_s[b] for b in range(MOE_NBLK)], axis=1)
        y = jnp.dot(act, wd_s[...], preferred_element_type=F32) + bd_ref[...]

        @pl.when(i >= 1)
        def _():
            wait_rows(ybuf, ssem, slot)

        _store_rows(ybuf.at[slot], y)

    @pl.when(i >= n_active)
    def _():
        @pl.when(i < last)
        def _():
            gather_and_scatter_rows()

        @pl.when(i == last)
        def _():
            issue_rows(scatter_row)

        wait_rows(ybuf, ssem, slot)

        @pl.when(i == last)
        def _():
            wait_rows(ybuf, ssem, oslot)


def _moe_call(layer, h_all, tile_expert, n_active, row_tok, row_dst, wgu, bgu_perm, wd, bd, perm):
    smem_rows = lambda f: pl.BlockSpec((None, 1, MOE_TM), f, memory_space=pltpu.SMEM)
    grid_spec = pltpu.PrefetchScalarGridSpec(
        num_scalar_prefetch=2,
        grid=(MOE_TILES,),
        in_specs=[
            smem_rows(lambda i, te, na: (i, 0, 0)),
            smem_rows(lambda i, te, na: (jnp.minimum(i + 1, MOE_TILES - 1), 0, 0)),
            smem_rows(lambda i, te, na: (jnp.where(i == 0, MOE_TILES, i - 1), 0, 0)),
            pl.BlockSpec(memory_space=pl.ANY),
            pl.BlockSpec((None, None, D_MODEL, 2 * D_MODEL), lambda i, te, na: (layer, te[i], 0, 0)),
            pl.BlockSpec((None, None, MOE_NBLK, 256), lambda i, te, na: (layer, te[i], 0, 0)),
            pl.BlockSpec((None, None, D_MODEL, D_MODEL), lambda i, te, na: (layer, te[i], 0, 0)),
            pl.BlockSpec((None, None, 1, D_MODEL), lambda i, te, na: (layer, te[i], 0, 0)),
            pl.BlockSpec((256, 256), lambda i, te, na: (0, 0)),
        ],
        out_specs=pl.BlockSpec(memory_space=pl.ANY),
        scratch_shapes=[
            pltpu.VMEM((2, MOE_TM * ROW_SUB, LANES), F32),
            pltpu.VMEM((2, MOE_TM * ROW_SUB, LANES), F32),
            pltpu.VMEM((MOE_NBLK, D_MODEL, 256), BF16),
            pltpu.VMEM((D_MODEL, D_MODEL), BF16),
            pltpu.VMEM((MOE_NBLK, MOE_TM, 128), BF16),
            pltpu.VMEM((MOE_TM, D_MODEL), BF16),
            pltpu.SemaphoreType.DMA((2,)),
            pltpu.SemaphoreType.DMA((2,)),
        ],
    )
    return pl.pallas_call(
        _moe_body,
        grid_spec=grid_spec,
        out_shape=jax.ShapeDtypeStruct((Y_ROWS * ROW_SUB, LANES), F32),
        compiler_params=_cp(("arbitrary",), 56 * 1024 * 1024),
        name="moe_experts",
    )(tile_expert, n_active, row_tok, row_tok, row_dst, h_all, wgu, bgu_perm, wd, bd, perm)


INV_GROUP = 16


def _inv_body(gs_ref, cn_ref, na_ref, dest_ref, src_hbm, src_s, sem):
    k = pl.program_id(0)

    def fill(lo, hi):
        def body(p, carry):
            src_s[p] = -1
            return carry
        lax.fori_loop(lo, hi, body, 0)

    @pl.when(k == 0)
    def _():
        end = na_ref[0] * MOE_TM
        for e in range(N_EXPERTS):
            fill(gs_ref[e] + cn_ref[e], gs_ref[e + 1] if e + 1 < N_EXPERTS else end)
        fill(end, P_PAD)

    def body(row, carry):
        val = k * T_ALL + row * LANES
        for c in range(0, LANES, INV_GROUP):
            dests = [dest_ref[row, c + u] for u in range(INV_GROUP)]
            for u in range(INV_GROUP):
                src_s[dests[u]] = val + (c + u)
        return carry
    lax.fori_loop(0, T_ALL // LANES, body, 0)

    @pl.when(k == TOP_K - 1)
    def _():
        cp = pltpu.make_async_copy(src_s, src_hbm, sem)
        cp.start()
        cp.wait()


def _inv_call(gstart, counts, n_active, dest):
    grid_spec = pltpu.PrefetchScalarGridSpec(
        num_scalar_prefetch=3,
        grid=(TOP_K,),
        in_specs=[pl.BlockSpec((None, T_ALL // LANES, LANES), lambda k, *_: (k, 0, 0), memory_space=pltpu.SMEM)],
        out_specs=pl.BlockSpec(memory_space=pl.ANY),
        scratch_shapes=[pltpu.SMEM((P_PAD,), jnp.int32), pltpu.SemaphoreType.DMA(())],
    )
    return pl.pallas_call(
        _inv_body,
        grid_spec=grid_spec,
        out_shape=jax.ShapeDtypeStruct((P_PAD,), jnp.int32),
        compiler_params=_cp(("arbitrary",)),
        name="route_inverse",
    )(gstart, counts, n_active, dest.reshape(TOP_K, T_ALL // LANES, LANES))


def _route(idx_t, rank_t, counts_f):
    counts = counts_f.reshape(N_EXPERTS).astype(jnp.int32)
    tiles = (counts + MOE_TM - 1) // MOE_TM
    tile_end = jnp.cumsum(tiles)
    gstart = (tile_end - tiles) * MOE_TM
    n_active = tile_end[N_EXPERTS - 1]
    ti = jnp.arange(MOE_TILES, dtype=jnp.int32)
    te = jnp.sum(tile_end[None, :] <= jnp.minimum(ti, n_active - 1)[:, None], axis=1, dtype=jnp.int32)
    te = jnp.minimum(te, N_EXPERTS - 1)
    eids = jnp.arange(N_EXPERTS, dtype=jnp.int32)
    dest = rank_t + jnp.sum(jnp.where(idx_t[:, :, None] == eids, gstart, 0), axis=-1, dtype=jnp.int32)
    n_active = n_active.reshape(1)
    src = _inv_call(gstart, counts, n_active, dest).reshape(P_PAD)
    valid = src >= 0
    k = src // T_ALL
    t = src - k * T_ALL
    p = jnp.arange(P_PAD, dtype=jnp.int32)
    dump = TOP_K * T_PAD + ((p // MOE_TM) % 2) * MOE_TM + p % MOE_TM
    row_tok = (jnp.where(valid, t, 0) * ROW_SUB).reshape(MOE_TILES, 1, MOE_TM)
    row_dst = (jnp.where(valid, k * T_PAD + t, dump) * ROW_SUB).reshape(MOE_TILES, 1, MOE_TM)
    first = ((TOP_K * T_PAD + MOE_TM + jnp.arange(MOE_TM, dtype=jnp.int32)) * ROW_SUB).reshape(1, 1, MOE_TM)
    return te, n_active, row_tok, jnp.concatenate([row_dst, first], axis=0)


def _weighted(p_ref, ys):
    acc = p_ref[:, 0:1] * _load_rows(ys[0])
    for k in range(1, TOP_K):
        acc = acc + p_ref[:, k:k + 1] * _load_rows(ys[k])
    return acc


def _combine_body(x_ref, g_ref, p_ref, y0, y1, y2, y3, o_ref):
    o_ref[...] = x_ref[...] + g_ref[...] * _weighted(p_ref, (y0, y1, y2, y3))


def _combine_norm_body(x_ref, g_ref, p_ref, y0, y1, y2, y3, nw_ref, o_ref):
    x = x_ref[...] + g_ref[...] * _weighted(p_ref, (y0, y1, y2, y3))
    o_ref[...] = x * lax.rsqrt(jnp.mean(x * x, axis=-1, keepdims=True) + EPS) * nw_ref[...]


def _combine(x, mod, layer, prr, y_all, tok0, tm, final_w=None, name="combine"):
    t = x.shape[0]
    rows = mod.shape[2]
    tps = (t // mod.shape[1]) // tm
    yspec = lambda k: pl.BlockSpec((tm * ROW_SUB, LANES), lambda i: ((k * T_PAD + tok0) // tm + i, 0))
    in_specs = [pl.BlockSpec((tm, D_MODEL), lambda i: (i, 0)), _mod_spec(rows, layer, 5, tps),
                pl.BlockSpec((tm, LANES), lambda i: (i, 0)), yspec(0), yspec(1), yspec(2), yspec(3)]
    args = [x, mod, prr, y_all, y_all, y_all, y_all]
    body = _combine_body
    if final_w is not None:
        in_specs.append(_resident((1, D_MODEL)))
        args.append(final_w)
        body = _combine_norm_body
    return pl.pallas_call(
        body,
        grid=(t // tm,),
        in_specs=in_specs,
        out_specs=pl.BlockSpec((tm, D_MODEL), lambda i: (i, 0)),
        out_shape=jax.ShapeDtypeStruct((t, D_MODEL), F32),
        compiler_params=_cp(("arbitrary",)),
        name=name,
    )(*args)


def _mixer_out_and_moe(layer, x_p, a_p, x_s, a_s, w_out, mod_p, mod_s, nw, rwt, rb, tm_p, tm_s,
                       moe_w_gate_up, bgu_perm, moe_w_down, moe_b_down, perm):
    zero_cnt = jnp.zeros((N_EXPERTS, 1), F32)
    x1_p, h_all, idx_p, rank_p, prr_p, cnt = _outproj(x_p, a_p, w_out, mod_p, layer, nw, rwt, rb, tm_p, zero_cnt,
                                                      None, "outproj%d_prompt" % layer)
    x1_s, h_all, idx_s, rank_s, prr_s, cnt = _outproj(x_s, a_s, w_out, mod_s, layer, nw, rwt, rb, tm_s, cnt,
                                                      h_all, "outproj%d_sample" % layer)
    idx_t = jnp.concatenate([idx_p, idx_s], axis=1)
    rank_t = jnp.concatenate([rank_p, rank_s], axis=1)
    te, n_active, row_tok, row_dst = _route(idx_t, rank_t, cnt)
    y_all = _moe_call(layer, h_all, te, n_active, row_tok, row_dst, moe_w_gate_up, bgu_perm,
                      moe_w_down, moe_b_down.reshape(-1, N_EXPERTS, 1, D_MODEL), perm)
    return x1_p, x1_s, prr_p, prr_s, y_all


def kernel(x_prompt, x_sample, c_prompt, c_sample, state_ssm, state_ssm_conv, state_sconv, ada_w, ada_b, norm1_w, norm2_w, ssd_w_in, ssd_conv_w, ssd_conv_b, ssd_dt_bias, ssd_A_log, ssd_D, ssd_norm_w, ssd_w_out, sc_w_in, sc_conv_w, sc_w_out, router_w, router_b, moe_w_gate_up, moe_b_gate_up, moe_w_down, moe_b_down, final_norm_w):
    tm_p = 256
    tm_s = DEC_BATCH
    depth = ada_w.shape[0]

    def hilo(w):
        hi = w.astype(BF16)
        return jnp.stack([hi, (w - hi.astype(F32)).astype(BF16)])

    w_in0 = hilo(jnp.concatenate(
        [ssd_w_in[:, SSD_INNER:SSD_INNER + SSD_CONV_DIM], ssd_w_in[:, :SSD_INNER],
         ssd_w_in[:, SSD_INNER + SSD_CONV_DIM:], jnp.zeros((D_MODEL, HEADS_PAD - SSD_HEADS), F32)],
        axis=1))
    w_out0 = hilo(ssd_w_out)
    w_in1 = sc_w_in.astype(BF16)
    w_out1 = sc_w_out.astype(BF16)
    pad_h = lambda v: jnp.concatenate([v, jnp.zeros((HEADS_PAD - SSD_HEADS,), F32)]).reshape(1, HEADS_PAD)
    dtb = pad_h(ssd_dt_bias)
    alog = pad_h(ssd_A_log)
    dch = jnp.repeat(ssd_D, SSD_HEAD_DIM).reshape(1, SSD_INNER)
    ssd_nw = ssd_norm_w.reshape(1, SSD_INNER)
    conv_b = ssd_conv_b.reshape(1, SSD_CONV_DIM)
    emat = (jnp.arange(SSD_INNER, dtype=jnp.int32)[None, :] // SSD_HEAD_DIM
            == jnp.arange(HEADS_PAD, dtype=jnp.int32)[:, None]).astype(BF16)
    jj = jnp.arange(256, dtype=jnp.int32)
    perm = (jj[None, :] == jnp.where(jj % 2 == 0, jj // 2, 128 + jj // 2)[:, None]).astype(BF16)
    bgu_perm = moe_b_gate_up.reshape(depth, N_EXPERTS, 8, 128, 2).transpose(0, 1, 2, 4, 3).reshape(
        depth, N_EXPERTS, MOE_NBLK, 256)
    rwt = jnp.transpose(router_w, (0, 2, 1))
    rb = router_b.reshape(depth, N_EXPERTS, 1)
    n1 = norm1_w.reshape(depth, 1, D_MODEL)
    n2 = norm2_w.reshape(depth, 1, D_MODEL)

    mod = _ada_call(jnp.concatenate([c_prompt, c_sample], axis=0), ada_w, ada_b)
    mod_p = mod[:, :BATCH].reshape(depth, BATCH, 1, 6 * D_MODEL)
    mod_s = mod[:, BATCH:].reshape(depth, 1, DEC_BATCH, 6 * D_MODEL)

    xp = x_prompt.reshape(T_PROMPT, D_MODEL)
    xs = x_sample.reshape(DEC_BATCH, D_MODEL)

    zx_p = _normmod_matmul(xp, n1[0], mod_p, 0, 1, 0, w_in0, tm_p, "inproj0_prompt")
    gn_p, ssm_p, ssm_conv_p = _ssd_prompt(zx_p, ssd_conv_w, conv_b, dtb, alog, dch, ssd_nw, emat)
    zx_s = _normmod_matmul(xs, n1[0], mod_s, 0, 1, 0, w_in0, tm_s, "inproj0_sample")
    xc_s, xdt_s, dec_s, ncs_s = _ssd_sample_prep(
        zx_s, state_ssm_conv.reshape(DEC_BATCH, (SSD_CONV - 1) * SSD_CONV_DIM), ssd_conv_w, conv_b, dtb, alog, emat)
    steps = DEC_BATCH // SAMPLE_BS
    to_cols = lambda a: a.reshape(steps, SAMPLE_BS, SSD_INNER).transpose(0, 2, 1)
    b3 = xc_s[:, SSD_INNER:SSD_INNER + SSD_GN].reshape(DEC_BATCH, SSD_GROUPS, SSD_STATE)
    c3 = xc_s[:, SSD_INNER + SSD_GN:].reshape(DEC_BATCH, SSD_GROUPS, SSD_STATE).transpose(1, 0, 2)
    ssm_s, gn_s = _ssd_sample_state(state_ssm, to_cols(xdt_s), dec_s, b3, c3, xc_s, zx_s, dch, ssd_nw)
    ssm_conv_s = ncs_s.reshape(DEC_BATCH, SSD_CONV - 1, SSD_CONV_DIM)

    x1_p, x1_s, prr_p, prr_s, y0 = _mixer_out_and_moe(
        0, xp, gn_p, xs, gn_s, w_out0, mod_p, mod_s, n2[0], rwt[0], rb[0], tm_p, tm_s,
        moe_w_gate_up, bgu_perm, moe_w_down, moe_b_down, perm)
    x2_p = _combine(x1_p, mod_p, 0, prr_p, y0, 0, tm_p, name="combine0_prompt")
    x2_s = _combine(x1_s, mod_s, 0, prr_s, y0, T_PROMPT, tm_s, name="combine0_sample")

    bcx_p = _normmod_matmul(x2_p, n1[1], mod_p, 1, 1, 0, w_in1, tm_p, "inproj1_prompt")
    v_p, sconv_p = _sconv_prompt(bcx_p, sc_conv_w)
    bcx_s = _normmod_matmul(x2_s, n1[1], mod_s, 1, 1, 0, w_in1, tm_s, "inproj1_sample")
    v_s, nsc_s = _sconv_sample(bcx_s, state_sconv.reshape(DEC_BATCH, (SC_WIDTH - 1) * D_MODEL), sc_conv_w)
    sconv_s = nsc_s.reshape(DEC_BATCH, SC_WIDTH - 1, D_MODEL)

    x3_p, x3_s, prr_p, prr_s, y1 = _mixer_out_and_moe(
        1, x2_p, v_p, x2_s, v_s, w_out1, mod_p, mod_s, n2[1], rwt[1], rb[1], tm_p, tm_s,
        moe_w_gate_up, bgu_perm, moe_w_down, moe_b_down, perm)
    fw = final_norm_w.reshape(1, D_MODEL)
    y_p = _combine(x3_p, mod_p, 1, prr_p, y1, 0, tm_p, final_w=fw, name="final_prompt")
    y_s = _combine(x3_s, mod_s, 1, prr_s, y1, T_PROMPT, tm_s, final_w=fw, name="final_sample")

    return (y_p.reshape(BATCH, SEQ, D_MODEL), y_s.reshape(DEC_BATCH, 1, D_MODEL), ssm_p, ssm_conv_p, sconv_p,
            ssm_s, ssm_conv_s, sconv_s)
```

```python
import functools

import jax
import jax.numpy as jnp
from jax import lax
from jax.experimental import pallas as pl
from jax.experimental.pallas import tpu as pltpu

F32 = jnp.float32
BF16 = jnp.bfloat16

D_MODEL = 1024
BATCH = 8
SEQ = 2048
DEC_BATCH = 128
SSD_INNER = 2048
SSD_HEAD_DIM = 64
SSD_HEADS = 32
SSD_GROUPS = 8
SSD_HPG = 4
SSD_STATE = 128
SSD_CONV = 4
SSD_CHUNK = 128
SSD_GN = 1024
SSD_CONV_DIM = 4096
SC_WIDTH = 3
N_EXPERTS = 32
TOP_K = 4
SWIGLU_LIMIT = 7.0
SWIGLU_ALPHA = 1.702
EPS = 1e-5

LANES = 128
HEADS_PAD = LANES
ZX_COLS = SSD_CONV_DIM + SSD_INNER + HEADS_PAD
T_PROMPT = BATCH * SEQ
T_ALL = T_PROMPT + DEC_BATCH
MOE_TM = 512
T_PAD = 16640
N_ASSIGN = TOP_K * T_ALL
MOE_TILES = (N_ASSIGN + N_EXPERTS * (MOE_TM - 1)) // MOE_TM + 1
P_PAD = MOE_TILES * MOE_TM
Y_ROWS = TOP_K * T_PAD + 2 * MOE_TM
VMEM_LIMIT = 48 * 1024 * 1024


def _cp(sem, vmem=VMEM_LIMIT):
    return pltpu.CompilerParams(dimension_semantics=sem, vmem_limit_bytes=vmem)


def _sigmoid(x):
    return 1.0 / (1.0 + jnp.exp(-x))


def _silu(x):
    return x * _sigmoid(x)


def _softplus(x):
    return jnp.maximum(x, 0.0) + jnp.log1p(jnp.exp(-jnp.abs(x)))


def _normmod(x, nw, sc, sh):
    y = x * lax.rsqrt(jnp.mean(x * x, axis=-1, keepdims=True) + EPS)
    return y * nw * (1.0 + sc) + sh


def _split3(x):
    hi = x.astype(BF16)
    r1 = x - hi.astype(F32)
    mid = r1.astype(BF16)
    lo = (r1 - mid.astype(F32)).astype(BF16)
    return hi, mid, lo


def _dot_exact_lhs(x, m01):
    hi, mid, lo = _split3(x)
    d = functools.partial(jnp.dot, preferred_element_type=F32)
    return d(hi, m01) + d(mid, m01) + d(lo, m01)


def _dot_exact_rhs(m01, x):
    hi, mid, lo = _split3(x)
    d = functools.partial(jnp.dot, preferred_element_type=F32)
    return d(m01, hi) + d(m01, mid) + d(m01, lo)


def _dot_nt(a, b):
    return lax.dot_general(a, b, (((1,), (1,)), ((), ())), preferred_element_type=F32)


def _split2(x):
    hi = x.astype(BF16)
    return hi, (x - hi.astype(F32)).astype(BF16)


def _dot_x3(a, b, nt=False):
    d = _dot_nt if nt else functools.partial(jnp.dot, preferred_element_type=F32)
    return d(a[0], b[0]) + d(a[1], b[0]) + d(a[0], b[1])


ROW_SUB = D_MODEL // LANES


def _load_rows(ref):
    tm = ref.shape[0] // ROW_SUB
    return jnp.concatenate([ref[pl.ds(c, tm, stride=ROW_SUB), :] for c in range(ROW_SUB)], axis=1)


def _store_rows(ref, x):
    tm = ref.shape[0] // ROW_SUB
    for c in range(ROW_SUB):
        ref[pl.ds(c, tm, stride=ROW_SUB), :] = x[:, c * LANES:(c + 1) * LANES]


def _ada_body(c_ref, w_ref, b_ref, o_ref):
    o_ref[...] = _dot_x3(_split2(_silu(c_ref[...])), _split2(w_ref[...])) + b_ref[...]


def _ada_call(c_all, ada_w, ada_b):
    n = c_all.shape[0]
    depth = ada_w.shape[0]
    return pl.pallas_call(
        _ada_body,
        grid=(depth, 6),
        in_specs=[
            pl.BlockSpec((n, D_MODEL), lambda l, j: (0, 0)),
            pl.BlockSpec((None, D_MODEL, D_MODEL), lambda l, j: (l, 0, j)),
            pl.BlockSpec((None, 1, D_MODEL), lambda l, j: (l, 0, j)),
        ],
        out_specs=pl.BlockSpec((None, n, D_MODEL), lambda l, j: (l, 0, j)),
        out_shape=jax.ShapeDtypeStruct((depth, n, 6 * D_MODEL), F32),
        compiler_params=_cp(("arbitrary", "arbitrary")),
        name="ada_mod",
    )(c_all, ada_w, ada_b.reshape(depth, 1, 6 * D_MODEL))


def _col_chunks(n, step=512):
    return [(s, min(step, n - s)) for s in range(0, n, step)]


def _rows_x3(a_hi, a_hilo, w_ref, s, n):
    tm = a_hi.shape[0]
    r = jnp.dot(a_hilo, w_ref[0, :, s:s + n], preferred_element_type=F32)
    return r[:tm] + r[tm:] + jnp.dot(a_hi, w_ref[1, :, s:s + n], preferred_element_type=F32)


def _nm_body(x_ref, nw_ref, sc_ref, sh_ref, w_ref, o_ref):
    h = _normmod(x_ref[...], nw_ref[...], sc_ref[...], sh_ref[...])
    if len(w_ref.shape) == 3:
        h_hi, h_lo = _split2(h)
        h_hilo = jnp.concatenate([h_hi, h_lo], axis=0)
        for s, n in _col_chunks(o_ref.shape[1]):
            o_ref[:, s:s + n] = _rows_x3(h_hi, h_hilo, w_ref, s, n)
    else:
        h = h.astype(BF16)
        for s, n in _col_chunks(o_ref.shape[1]):
            o_ref[:, s:s + n] = jnp.dot(h, w_ref[:, s:s + n], preferred_element_type=F32)


def _mod_spec(rows, layer, col, tiles_per_seq):
    return pl.BlockSpec((None, None, rows, D_MODEL), lambda i: (layer, i // tiles_per_seq, 0, col))


def _resident(shape):
    return pl.BlockSpec(shape, lambda *_: tuple(0 for _ in shape), pipeline_mode=pl.Buffered(1))


def _normmod_matmul(x, nw, mod, layer, sc_col, sh_col, w_bf, tm, name):
    t = x.shape[0]
    n = w_bf.shape[-1]
    rows = mod.shape[2]
    tps = (t // mod.shape[1]) // tm
    return pl.pallas_call(
        _nm_body,
        grid=(t // tm,),
        in_specs=[
            pl.BlockSpec((tm, D_MODEL), lambda i: (i, 0)),
            _resident((1, D_MODEL)),
            _mod_spec(rows, layer, sc_col, tps),
            _mod_spec(rows, layer, sh_col, tps),
            _resident(w_bf.shape),
        ],
        out_specs=pl.BlockSpec((tm, n), lambda i: (i, 0)),
        out_shape=jax.ShapeDtypeStruct((t, n), F32),
        compiler_params=_cp(("arbitrary",)),
        name=name,
    )(x, nw, mod, mod, w_bf)


def _ssd_prompt_body(xbc_ref, z_ref, dt_ref, cw_ref, cb_ref, dtb_ref, alog_ref, dch_ref, nw_ref, e_ref,
                     gn_ref, st_ref, cs_ref, cbuf, ht, xc_s, yd_s):
    q = SSD_CHUNK
    c = pl.program_id(1)

    @pl.when(c == 0)
    def _():
        cbuf[0:8, :] = jnp.zeros((8, SSD_CONV_DIM), F32)
        ht[...] = jnp.zeros_like(ht)

    cbuf[8:8 + q, :] = xbc_ref[...]
    for s, n in _col_chunks(SSD_CONV_DIM):
        acc = cbuf[5:5 + q, s:s + n] * cw_ref[0:1, s:s + n]
        for k in range(1, SSD_CONV):
            acc = acc + cbuf[5 + k:5 + k + q, s:s + n] * cw_ref[k:k + 1, s:s + n]
        xc_s[:, s:s + n] = _silu(acc + cb_ref[:, s:s + n])
    cs_ref[...] = cbuf[5 + q:8 + q, :]
    cbuf[0:8, :] = cbuf[q:q + 8, :]

    dt = _softplus(dt_ref[...] + dtb_ref[...])
    da = dt * (-jnp.exp(alog_ref[...]))
    ri = lax.broadcasted_iota(jnp.int32, (q, q), 0)
    ci = lax.broadcasted_iota(jnp.int32, (q, q), 1)
    causal = ri >= ci
    tri = jnp.where(causal, 1.0, 0.0).astype(BF16)
    tri_t = jnp.where(ri <= ci, 1.0, 0.0).astype(BF16)
    acs = _dot_exact_rhs(tri, da)
    acs_t = _dot_exact_lhs(da.T, tri_t)

    for g in range(SSD_GROUPS):
        cs = slice(g * 256, (g + 1) * 256)
        e_g = e_ref[:, cs]
        dte = _dot_exact_lhs(dt, e_g)
        ace = _dot_exact_lhs(acs, e_g)
        last = ace[q - 1:q, :]
        xs_g = xc_s[:, cs]
        xdt = xs_g * dte
        b_g = xc_s[:, SSD_INNER + g * SSD_STATE:SSD_INNER + (g + 1) * SSD_STATE]
        c_g = xc_s[:, SSD_INNER + SSD_GN + g * SSD_STATE:SSD_INNER + SSD_GN + (g + 1) * SSD_STATE]
        c_2 = _split2(c_g)
        cb = _dot_x3(c_2, _split2(b_g), nt=True)
        h_prev = ht[g]
        y_off = _dot_x3(c_2, _split2(h_prev)) * jnp.exp(ace)
        st_t = _dot_x3(_split2(b_g.T), _split2(jnp.exp(last - ace) * xdt))
        ht[g] = jnp.exp(last) * h_prev + st_t
        xdt_hi, xdt_lo = _split2(xdt)
        for r in range(SSD_HPG):
            h = g * SSD_HPG + r
            hs = slice(r * 64, (r + 1) * 64)
            diff = acs[:, h:h + 1] - acs_t[h:h + 1, :]
            m_2 = _split2(cb * jnp.where(causal, jnp.exp(diff), 0.0))
            yd_s[:, hs] = _dot_x3(m_2, (xdt_hi[:, hs], xdt_lo[:, hs]))
        y = yd_s[...] + y_off + xs_g * dch_ref[:, cs]
        gg = y * _silu(z_ref[:, cs])
        ms = jnp.mean(gg * gg, axis=-1, keepdims=True)
        gn_ref[:, cs] = gg * lax.rsqrt(ms + EPS) * nw_ref[:, cs]

    @pl.when(c == pl.num_programs(1) - 1)
    def _():
        for g in range(SSD_GROUPS):
            t = ht[g].T
            for r in range(SSD_HPG):
                st_ref[g * SSD_HPG + r] = t[r * 64:(r + 1) * 64, :]


def _ssd_prompt(zx, conv_w, conv_b, dtb, alog, dch, nw, emat):
    nc = SEQ // SSD_CHUNK
    q = SSD_CHUNK
    row = lambda b, c: b * nc + c
    return pl.pallas_call(
        _ssd_prompt_body,
        grid=(BATCH, nc),
        in_specs=[
            pl.BlockSpec((q, SSD_CONV_DIM), lambda b, c: (row(b, c), 0)),
            pl.BlockSpec((q, SSD_INNER), lambda b, c: (row(b, c), 2)),
            pl.BlockSpec((q, HEADS_PAD), lambda b, c: (row(b, c), (SSD_CONV_DIM + SSD_INNER) // HEADS_PAD)),
            _resident((SSD_CONV, SSD_CONV_DIM)),
            _resident((1, SSD_CONV_DIM)),
            _resident((1, HEADS_PAD)),
            _resident((1, HEADS_PAD)),
            _resident((1, SSD_INNER)),
            _resident((1, SSD_INNER)),
            _resident((HEADS_PAD, SSD_INNER)),
        ],
        out_specs=[
            pl.BlockSpec((q, SSD_INNER), lambda b, c: (row(b, c), 0)),
            pl.BlockSpec((None, SSD_HEADS, SSD_HEAD_DIM, SSD_STATE), lambda b, c: (b, 0, 0, 0)),
            pl.BlockSpec((None, SSD_CONV - 1, SSD_CONV_DIM), lambda b, c: (b, 0, 0)),
        ],
        out_shape=[
            jax.ShapeDtypeStruct((T_PROMPT, SSD_INNER), F32),
            jax.ShapeDtypeStruct((BATCH, SSD_HEADS, SSD_HEAD_DIM, SSD_STATE), F32),
            jax.ShapeDtypeStruct((BATCH, SSD_CONV - 1, SSD_CONV_DIM), F32),
        ],
        scratch_shapes=[
            pltpu.VMEM((8 + q, SSD_CONV_DIM), F32),
            pltpu.VMEM((SSD_GROUPS, SSD_STATE, 256), F32),
            pltpu.VMEM((q, SSD_CONV_DIM), F32),
            pltpu.VMEM((q, 256), F32),
        ],
        compiler_params=_cp(("arbitrary", "arbitrary")),
        name="ssd_prompt",
    )(zx, zx, zx, conv_w, conv_b, dtb, alog, dch, nw, emat)


def _ssd_sample_prep_body(zx_ref, cst_ref, cw_ref, cb_ref, dtb_ref, alog_ref, e_ref,
                          xc_ref, xdt_ref, dec_ref, ncs_ref):
    kc = SSD_CONV_DIM
    for s, n in _col_chunks(kc):
        xnew = zx_ref[:, s:s + n]
        acc = xnew * cw_ref[SSD_CONV - 1:SSD_CONV, s:s + n]
        for k in range(SSD_CONV - 1):
            acc = acc + cst_ref[:, k * kc + s:k * kc + s + n] * cw_ref[k:k + 1, s:s + n]
        xc_ref[:, s:s + n] = _silu(acc + cb_ref[:, s:s + n])
        for k in range(SSD_CONV - 2):
            ncs_ref[:, k * kc + s:k * kc + s + n] = cst_ref[:, (k + 1) * kc + s:(k + 1) * kc + s + n]
        ncs_ref[:, (SSD_CONV - 2) * kc + s:(SSD_CONV - 2) * kc + s + n] = xnew
    dt = _softplus(zx_ref[:, SSD_CONV_DIM + SSD_INNER:] + dtb_ref[...])
    dec_ref[...] = jnp.exp(dt * (-jnp.exp(alog_ref[...])))
    for s, n in _col_chunks(SSD_INNER):
        xdt_ref[:, s:s + n] = xc_ref[:, s:s + n] * _dot_exact_lhs(dt, e_ref[:, s:s + n])


def _ssd_sample_prep(zx, cst, conv_w, conv_b, dtb, alog, emat):
    n = DEC_BATCH
    full = lambda shape: pl.BlockSpec(shape, lambda i: tuple(0 for _ in shape))
    return pl.pallas_call(
        _ssd_sample_prep_body,
        grid=(1,),
        in_specs=[full((n, ZX_COLS)), full((n, (SSD_CONV - 1) * SSD_CONV_DIM)), full((SSD_CONV, SSD_CONV_DIM)),
                  full((1, SSD_CONV_DIM)), full((1, HEADS_PAD)), full((1, HEADS_PAD)),
                  full((HEADS_PAD, SSD_INNER))],
        out_specs=[full((n, SSD_CONV_DIM)), full((n, SSD_INNER)), full((n, HEADS_PAD)),
                   full((n, (SSD_CONV - 1) * SSD_CONV_DIM))],
        out_shape=[
            jax.ShapeDtypeStruct((n, SSD_CONV_DIM), F32),
            jax.ShapeDtypeStruct((n, SSD_INNER), F32),
            jax.ShapeDtypeStruct((n, HEADS_PAD), F32),
            jax.ShapeDtypeStruct((n, (SSD_CONV - 1) * SSD_CONV_DIM), F32),
        ],
        compiler_params=_cp(("arbitrary",)),
        name="ssd_sample_prep",
    )(zx, cst, conv_w, conv_b, dtb, alog, emat)


SAMPLE_BS = 8


def _ssd_sample_state_body(dec_ref, st_ref, xt_ref, b_ref, c_ref, xs_ref, z_ref, dch_ref, nw_ref,
                           so_ref, gn_ref, y_s):
    def group(g, carry):
        r0 = pl.multiple_of(g * 256, 256)
        c_2 = _split2(c_ref[g])
        for i in range(SAMPLE_BS):
            h = st_ref[i, pl.ds(g * SSD_HPG, SSD_HPG)]
            hd = jnp.concatenate([h[r] * dec_ref[i, g * SSD_HPG + r] for r in range(SSD_HPG)], axis=0)
            xcol = xt_ref[pl.ds(r0, 256), i:i + 1]
            brow = b_ref[i, pl.ds(g, 1), :]
            hn = hd + xcol * brow
            so_ref[i, pl.ds(g * SSD_HPG, SSD_HPG)] = hn.reshape(SSD_HPG, SSD_HEAD_DIM, SSD_STATE)
            yall = _dot_x3(c_2, _split2(hn), nt=True)
            y_s[g, i:i + 1, :] = yall[i:i + 1, :]
        return carry
    lax.fori_loop(0, SSD_GROUPS, group, 0)
    for g in range(SSD_GROUPS):
        cs = slice(g * 256, (g + 1) * 256)
        y = y_s[g] + xs_ref[:, cs] * dch_ref[:, cs]
        gg = y * _silu(z_ref[:, cs])
        ms = jnp.mean(gg * gg, axis=-1, keepdims=True)
        gn_ref[:, cs] = gg * lax.rsqrt(ms + EPS) * nw_ref[:, cs]


def _ssd_sample_state(state, xt, dec, b3, c3, xc, zx, dch, nw):
    bs = SAMPLE_BS
    steps = DEC_BATCH // bs
    return pl.pallas_call(
        _ssd_sample_state_body,
        grid=(steps,),
        in_specs=[
            pl.BlockSpec((None, bs, SSD_HEADS), lambda s: (s, 0, 0), memory_space=pltpu.SMEM),
            pl.BlockSpec((bs, SSD_HEADS, SSD_HEAD_DIM, SSD_STATE), lambda s: (s, 0, 0, 0)),
            pl.BlockSpec((None, SSD_INNER, bs), lambda s: (s, 0, 0)),
            pl.BlockSpec((bs, SSD_GROUPS, SSD_STATE), lambda s: (s, 0, 0)),
            pl.BlockSpec((SSD_GROUPS, bs, SSD_STATE), lambda s: (0, s, 0)),
            pl.BlockSpec((bs, SSD_INNER), lambda s: (s, 0)),
            pl.BlockSpec((bs, SSD_INNER), lambda s: (s, 2)),
            _resident((1, SSD_INNER)),
            _resident((1, SSD_INNER)),
        ],
        out_specs=[
            pl.BlockSpec((bs, SSD_HEADS, SSD_HEAD_DIM, SSD_STATE), lambda s: (s, 0, 0, 0)),
            pl.BlockSpec((bs, SSD_INNER), lambda s: (s, 0)),
        ],
        out_shape=[
            jax.ShapeDtypeStruct(state.shape, F32),
            jax.ShapeDtypeStruct((DEC_BATCH, SSD_INNER), F32),
        ],
        scratch_shapes=[pltpu.VMEM((SSD_GROUPS, bs, 256), F32)],
        compiler_params=_cp(("arbitrary",), 56 * 1024 * 1024),
        name="ssd_sample_state",
    )(dec[:, :SSD_HEADS].reshape(steps, bs, SSD_HEADS), state, xt, b3, c3, xc, zx, dch, nw)


def _sconv_prompt_body(bg_ref, cg_ref, xh_ref, cw_ref, v_ref, st_ref, cbuf):
    tm = bg_ref.shape[0]
    j = pl.program_id(1)

    @pl.when(j == 0)
    def _():
        cbuf[0:8, :] = jnp.zeros((8, D_MODEL), F32)

    cbuf[8:8 + tm, :] = cg_ref[...] * xh_ref[...]
    acc = cbuf[6:6 + tm, :] * cw_ref[0:1, :]
    for k in range(1, SC_WIDTH):
        acc = acc + cbuf[6 + k:6 + k + tm, :] * cw_ref[k:k + 1, :]
    v_ref[...] = (bg_ref[...] * acc).astype(BF16)
    st_ref[...] = cbuf[6 + tm:8 + tm, :]
    cbuf[0:8, :] = cbuf[tm:tm + 8, :]


def _sconv_prompt(bcx, conv_w, tm=256):
    nt = SEQ // tm
    row = lambda b, j: b * nt + j
    return pl.pallas_call(
        _sconv_prompt_body,
        grid=(BATCH, nt),
        in_specs=[
            pl.BlockSpec((tm, D_MODEL), lambda b, j: (row(b, j), 0)),
            pl.BlockSpec((tm, D_MODEL), lambda b, j: (row(b, j), 1)),
            pl.BlockSpec((tm, D_MODEL), lambda b, j: (row(b, j), 2)),
            _resident((SC_WIDTH, D_MODEL)),
        ],
        out_specs=[
            pl.BlockSpec((tm, D_MODEL), lambda b, j: (row(b, j), 0)),
            pl.BlockSpec((None, SC_WIDTH - 1, D_MODEL), lambda b, j: (b, 0, 0)),
        ],
        out_shape=[
            jax.ShapeDtypeStruct((T_PROMPT, D_MODEL), BF16),
            jax.ShapeDtypeStruct((BATCH, SC_WIDTH - 1, D_MODEL), F32),
        ],
        scratch_shapes=[pltpu.VMEM((8 + tm, D_MODEL), F32)],
        compiler_params=_cp(("arbitrary", "arbitrary")),
        name="sconv_prompt",
    )(bcx, bcx, bcx, conv_w)


def _sconv_sample_body(bcx_ref, st_ref, cw_ref, v_ref, ns_ref):
    d = D_MODEL
    u = bcx_ref[:, d:2 * d] * bcx_ref[:, 2 * d:3 * d]
    acc = u * cw_ref[SC_WIDTH - 1:SC_WIDTH, :]
    for k in range(SC_WIDTH - 1):
        acc = acc + st_ref[:, k * d:(k + 1) * d] * cw_ref[k:k + 1, :]
    v_ref[...] = (bcx_ref[:, 0:d] * acc).astype(BF16)
    for k in range(SC_WIDTH - 2):
        ns_ref[:, k * d:(k + 1) * d] = st_ref[:, (k + 1) * d:(k + 2) * d]
    ns_ref[:, (SC_WIDTH - 2) * d:] = u


def _sconv_sample(bcx, st, conv_w):
    n = DEC_BATCH
    full = lambda shape: pl.BlockSpec(shape, lambda i: tuple(0 for _ in shape))
    return pl.pallas_call(
        _sconv_sample_body,
        grid=(1,),
        in_specs=[full((n, 3 * D_MODEL)), full((n, (SC_WIDTH - 1) * D_MODEL)), full((SC_WIDTH, D_MODEL))],
        out_specs=[full((n, D_MODEL)), full((n, (SC_WIDTH - 1) * D_MODEL))],
        out_shape=[jax.ShapeDtypeStruct((n, D_MODEL), BF16),
                   jax.ShapeDtypeStruct((n, (SC_WIDTH - 1) * D_MODEL), F32)],
        compiler_params=_cp(("arbitrary",)),
        name="sconv_sample",
    )(bcx, st, conv_w)


def _outproj_body(aliased, x_ref, a_ref, w_ref, g1_ref, nw_ref, sc_ref, sh_ref, rwt_ref, rb_ref, u_ref, cin_ref,
                  *refs):
    x1_ref, h2_ref, idx_ref, rank_ref, prr_ref, cnt_ref, run = refs[1:] if aliased else refs

    @pl.when(pl.program_id(0) == 0)
    def _():
        run[...] = cin_ref[...]

    if len(w_ref.shape) == 3:
        a_hi, a_lo = _split2(a_ref[...])
        m = _rows_x3(a_hi, jnp.concatenate([a_hi, a_lo], axis=0), w_ref, 0, D_MODEL)
    else:
        m = jnp.dot(a_ref[...], w_ref[...], preferred_element_type=F32)
    x1 = x_ref[...] + g1_ref[...] * m
    x1_ref[...] = x1
    h2 = _normmod(x1, nw_ref[...], sc_ref[...], sh_ref[...])
    _store_rows(h2_ref, h2)
    h_hi = h2.astype(BF16)
    h_lo = (h2 - h_hi.astype(F32)).astype(BF16)
    rw = rwt_ref[...]
    w_hi = rw.astype(BF16)
    w_lo = (rw - w_hi.astype(F32)).astype(BF16)
    logits = _dot_nt(w_hi, h_hi) + _dot_nt(w_hi, h_lo) + _dot_nt(w_lo, h_hi) + rb_ref[...]
    eio = lax.broadcasted_iota(jnp.int32, logits.shape, 0)
    vals, idxs = [], []
    for _ in range(TOP_K):
        mx = jnp.max(logits, axis=0, keepdims=True)
        sel = jnp.min(jnp.where(logits == mx, eio, N_EXPERTS), axis=0, keepdims=True)
        vals.append(mx)
        idxs.append(sel)
        logits = jnp.where(eio == sel, -jnp.inf, logits)
    ex = [jnp.exp(v - vals[0]) for v in vals]
    tot = ex[0] + ex[1] + ex[2] + ex[3]
    idx_ref[...] = jnp.concatenate(idxs, axis=0)
    tm = logits.shape[1]
    base = run[...]
    ranks = []
    for k in range(TOP_K):
        ohf = jnp.where(eio == idxs[k], 1.0, 0.0)
        pref = jnp.dot(ohf.astype(BF16), u_ref[...], preferred_element_type=F32)
        ranks.append(jnp.sum(ohf * (pref - 1.0 + base), axis=0, keepdims=True))
        base = base + pref[:, tm - 1:tm]
    run[...] = base
    cnt_ref[...] = base
    rank_ref[...] = jnp.concatenate(ranks, axis=0).astype(jnp.int32)
    pr = jnp.concatenate([e / tot for e in ex] + [jnp.zeros((LANES - TOP_K, tm), F32)], axis=0)
    prr_ref[...] = pr.T


def _outproj(x, a, w_bf, mod, layer, nw, rwt, rb, tm, cnt_in, h_buf, name):
    t = x.shape[0]
    k = a.shape[1]
    rows = mod.shape[2]
    tps = (t // mod.shape[1]) // tm
    aliased = h_buf is not None
    tok0 = T_PROMPT if aliased else 0
    umat = (jnp.arange(tm, dtype=jnp.int32)[:, None] <= jnp.arange(tm, dtype=jnp.int32)[None, :]).astype(BF16)
    in_specs = [
        pl.BlockSpec((tm, D_MODEL), lambda i: (i, 0)),
        pl.BlockSpec((tm, k), lambda i: (i, 0)),
        _resident(w_bf.shape),
        _mod_spec(rows, layer, 2, tps),
        _resident((1, D_MODEL)),
        _mod_spec(rows, layer, 4, tps),
        _mod_spec(rows, layer, 3, tps),
        _resident((N_EXPERTS, D_MODEL)),
        _resident((N_EXPERTS, 1)),
        _resident((tm, tm)),
        _resident((N_EXPERTS, 1)),
    ]
    args = [x, a, w_bf, mod, nw, mod, mod, rwt, rb, umat, cnt_in]
    if aliased:
        in_specs.append(pl.BlockSpec(memory_space=pl.ANY))
        args.append(h_buf)
    return pl.pallas_call(
        functools.partial(_outproj_body, aliased),
        grid=(t // tm,),
        in_specs=in_specs,
        out_specs=[
            pl.BlockSpec((tm, D_MODEL), lambda i: (i, 0)),
            pl.BlockSpec((tm * ROW_SUB, LANES), lambda i: (tok0 // tm + i, 0)),
            pl.BlockSpec((TOP_K, tm), lambda i: (0, i)),
            pl.BlockSpec((TOP_K, tm), lambda i: (0, i)),
            pl.BlockSpec((tm, LANES), lambda i: (i, 0)),
            pl.BlockSpec((N_EXPERTS, 1), lambda i: (0, 0)),
        ],
        out_shape=[
            jax.ShapeDtypeStruct((t, D_MODEL), F32),
            jax.ShapeDtypeStruct((T_ALL * ROW_SUB, LANES), F32),
            jax.ShapeDtypeStruct((TOP_K, t), jnp.int32),
            jax.ShapeDtypeStruct((TOP_K, t), jnp.int32),
            jax.ShapeDtypeStruct((t, LANES), F32),
            jax.ShapeDtypeStruct((N_EXPERTS, 1), F32),
        ],
        scratch_shapes=[pltpu.VMEM((N_EXPERTS, 1), F32)],
        input_output_aliases={len(args) - 1: 1} if aliased else {},
        compiler_params=_cp(("arbitrary",)),
        name=name,
    )(*args)


MOE_NBLK = 8


def _moe_body(te_ref, na_ref, tok_ref, tokn_ref, dstp_ref, h_hbm, wgu_ref, bgu_ref, wd_ref, bd_ref,
              perm_ref, y_hbm, xbuf, ybuf, wgu_s, wd_s, act_s, gsem, ssem):
    i = pl.program_id(0)
    last = MOE_TILES - 1
    n_active = na_ref[0]
    slot = i % 2
    oslot = 1 - slot

    unroll = 16

    def issue_rows(start_row):
        def body(j, carry):
            for u in range(unroll):
                start_row(j * unroll + u)
            return carry
        lax.fori_loop(0, MOE_TM // unroll, body, 0)

    def hbm_row(ref, idx):
        return ref.at[pl.ds(pl.multiple_of(idx, ROW_SUB), ROW_SUB)]

    def vmem_row(buf, sl, r):
        return buf.at[sl, pl.ds(pl.multiple_of(r * ROW_SUB, ROW_SUB), ROW_SUB)]

    def gather_row(rows_ref, sl, r):
        pltpu.make_async_copy(hbm_row(h_hbm, rows_ref[0, r]), vmem_row(xbuf, sl, r), gsem.at[sl]).start(priority=0)

    def scatter_row(r):
        pltpu.make_async_copy(vmem_row(ybuf, oslot, r), hbm_row(y_hbm, dstp_ref[0, r]),
                              ssem.at[oslot]).start(priority=1)

    def gather_rows(rows_ref, sl):
        issue_rows(lambda r: gather_row(rows_ref, sl, r))

    def gather_and_scatter_rows():
        def both(r):
            gather_row(tokn_ref, oslot, r)
            scatter_row(r)
        issue_rows(both)

    def wait_rows(buf, sem, sl):
        pltpu.make_async_copy(buf.at[sl], buf.at[sl], sem.at[sl]).wait()

    @pl.when(i == 0)
    def _():
        gather_rows(tok_ref, 0)
        ybuf[...] = jnp.zeros_like(ybuf)
        pad = (T_PAD - T_ALL) * ROW_SUB
        fills = [pltpu.make_async_copy(ybuf.at[0, pl.ds(0, pad)],
                                       y_hbm.at[pl.ds((k * T_PAD + T_ALL) * ROW_SUB, pad)], ssem.at[0])
                 for k in range(TOP_K)]
        fills += [pltpu.make_async_copy(ybuf.at[0],
                                        y_hbm.at[pl.ds((TOP_K * T_PAD + d * MOE_TM) * ROW_SUB, MOE_TM * ROW_SUB)],
                                        ssem.at[0]) for d in range(2)]
        for f in fills:
            f.start()
        for f in fills:
            f.wait()

    @pl.when(i < last)
    def _():
        gather_and_scatter_rows()

    @pl.when(i == last)
    def _():
        issue_rows(scatter_row)

    wait_rows(xbuf, gsem, slot)

    @pl.when(i < n_active)
    def _():
        e_prev = te_ref[jnp.maximum(i - 1, 0)]

        @pl.when(jnp.logical_or(i == 0, e_prev != te_ref[i]))
        def _():
            for b in range(MOE_NBLK):
                wgu_s[b] = jnp.dot(wgu_ref[:, b * 256:(b + 1) * 256].astype(BF16), perm_ref[...],
                                   preferred_element_type=F32).astype(BF16)
            wd_s[...] = wd_ref[...].astype(BF16)

        x = _load_rows(xbuf.at[slot]).astype(BF16)
        for b in range(MOE_NBLK):
            gu = jnp.dot(x, wgu_s[b], preferred_element_type=F32) + bgu_ref[b:b + 1, :]
            gate = jnp.minimum(gu[:, :128], SWIGLU_LIMIT)
            up = jnp.clip(gu[:, 128:], -SWIGLU_LIMIT, SWIGLU_LIMIT)
            act_s[b] = ((up + 1.0) * (gate * _sigmoid(SWIGLU_ALPHA * gate))).astype(BF16)
        act = jnp.concatenate([act_s[b] for b in range(MOE_NBLK)], axis=1)
        y = jnp.dot(act, wd_s[...], preferred_element_type=F32) + bd_ref[...]

        @pl.when(i >= 1)
        def _():
            wait_rows(ybuf, ssem, slot)

        _store_rows(ybuf.at[slot], y)

    @pl.when(i >= n_active)
    def _():
        wait_rows(ybuf, ssem, slot)

        @pl.when(i == last)
        def _():
            wait_rows(ybuf, ssem, oslot)


def _moe_call(layer, h_all, tile_expert, n_active, row_tok, row_dst, wgu, bgu_perm, wd, bd, perm):
    smem_rows = lambda f: pl.BlockSpec((None, 1, MOE_TM), f, memory_space=pltpu.SMEM)
    grid_spec = pltpu.PrefetchScalarGridSpec(
        num_scalar_prefetch=2,
        grid=(MOE_TILES,),
        in_specs=[
            smem_rows(lambda i, te, na: (i, 0, 0)),
            smem_rows(lambda i, te, na: (jnp.minimum(i + 1, MOE_TILES - 1), 0, 0)),
            smem_rows(lambda i, te, na: (jnp.where(i == 0, MOE_TILES, i - 1), 0, 0)),
            pl.BlockSpec(memory_space=pl.ANY),
            pl.BlockSpec((None, None, D_MODEL, 2 * D_MODEL), lambda i, te, na: (layer, te[i], 0, 0)),
            pl.BlockSpec((None, None, MOE_NBLK, 256), lambda i, te, na: (layer, te[i], 0, 0)),
            pl.BlockSpec((None, None, D_MODEL, D_MODEL), lambda i, te, na: (layer, te[i], 0, 0)),
            pl.BlockSpec((None, None, 1, D_MODEL), lambda i, te, na: (layer, te[i], 0, 0)),
            pl.BlockSpec((256, 256), lambda i, te, na: (0, 0)),
        ],
        out_specs=pl.BlockSpec(memory_space=pl.ANY),
        scratch_shapes=[
            pltpu.VMEM((2, MOE_TM * ROW_SUB, LANES), F32),
            pltpu.VMEM((2, MOE_TM * ROW_SUB, LANES), F32),
            pltpu.VMEM((MOE_NBLK, D_MODEL, 256), BF16),
            pltpu.VMEM((D_MODEL, D_MODEL), BF16),
            pltpu.VMEM((MOE_NBLK, MOE_TM, 128), BF16),
            pltpu.SemaphoreType.DMA((2,)),
            pltpu.SemaphoreType.DMA((2,)),
        ],
    )
    return pl.pallas_call(
        _moe_body,
        grid_spec=grid_spec,
        out_shape=jax.ShapeDtypeStruct((Y_ROWS * ROW_SUB, LANES), F32),
        compiler_params=_cp(("arbitrary",), 56 * 1024 * 1024),
        name="moe_experts",
    )(tile_expert, n_active, row_tok, row_tok, row_dst, h_all, wgu, bgu_perm, wd, bd, perm)


INV_GROUP = 16


def _inv_body(gs_ref, cn_ref, na_ref, dest_ref, src_hbm, src_s, sem):
    k = pl.program_id(0)

    def fill(lo, hi):
        def body(p, carry):
            src_s[p] = -1
            return carry
        lax.fori_loop(lo, hi, body, 0)

    @pl.when(k == 0)
    def _():
        end = na_ref[0] * MOE_TM
        for e in range(N_EXPERTS):
            fill(gs_ref[e] + cn_ref[e], gs_ref[e + 1] if e + 1 < N_EXPERTS else end)
        fill(end, P_PAD)

    def body(row, carry):
        val = k * T_ALL + row * LANES
        for c in range(0, LANES, INV_GROUP):
            dests = [dest_ref[row, c + u] for u in range(INV_GROUP)]
            for u in range(INV_GROUP):
                src_s[dests[u]] = val + (c + u)
        return carry
    lax.fori_loop(0, T_ALL // LANES, body, 0)

    @pl.when(k == TOP_K - 1)
    def _():
        cp = pltpu.make_async_copy(src_s, src_hbm, sem)
        cp.start()
        cp.wait()


def _inv_call(gstart, counts, n_active, dest):
    grid_spec = pltpu.PrefetchScalarGridSpec(
        num_scalar_prefetch=3,
        grid=(TOP_K,),
        in_specs=[pl.BlockSpec((None, T_ALL // LANES, LANES), lambda k, *_: (k, 0, 0), memory_space=pltpu.SMEM)],
        out_specs=pl.BlockSpec(memory_space=pl.ANY),
        scratch_shapes=[pltpu.SMEM((P_PAD,), jnp.int32), pltpu.SemaphoreType.DMA(())],
    )
    return pl.pallas_call(
        _inv_body,
        grid_spec=grid_spec,
        out_shape=jax.ShapeDtypeStruct((P_PAD,), jnp.int32),
        compiler_params=_cp(("arbitrary",)),
        name="route_inverse",
    )(gstart, counts, n_active, dest.reshape(TOP_K, T_ALL // LANES, LANES))


def _route(idx_t, rank_t, counts_f):
    counts = counts_f.reshape(N_EXPERTS).astype(jnp.int32)
    tiles = (counts + MOE_TM - 1) // MOE_TM
    tile_end = jnp.cumsum(tiles)
    gstart = (tile_end - tiles) * MOE_TM
    n_active = tile_end[N_EXPERTS - 1]
    ti = jnp.arange(MOE_TILES, dtype=jnp.int32)
    te = jnp.sum(tile_end[None, :] <= jnp.minimum(ti, n_active - 1)[:, None], axis=1, dtype=jnp.int32)
    te = jnp.minimum(te, N_EXPERTS - 1)
    eids = jnp.arange(N_EXPERTS, dtype=jnp.int32)
    dest = rank_t + jnp.sum(jnp.where(idx_t[:, :, None] == eids, gstart, 0), axis=-1, dtype=jnp.int32)
    n_active = n_active.reshape(1)
    src = _inv_call(gstart, counts, n_active, dest).reshape(P_PAD)
    valid = src >= 0
    k = src // T_ALL
    t = src - k * T_ALL
    p = jnp.arange(P_PAD, dtype=jnp.int32)
    dump = TOP_K * T_PAD + ((p // MOE_TM) % 2) * MOE_TM + p % MOE_TM
    row_tok = (jnp.where(valid, t, 0) * ROW_SUB).reshape(MOE_TILES, 1, MOE_TM)
    row_dst = (jnp.where(valid, k * T_PAD + t, dump) * ROW_SUB).reshape(MOE_TILES, 1, MOE_TM)
    first = ((TOP_K * T_PAD + MOE_TM + jnp.arange(MOE_TM, dtype=jnp.int32)) * ROW_SUB).reshape(1, 1, MOE_TM)
    return te, n_active, row_tok, jnp.concatenate([row_dst, first], axis=0)


def _weighted(p_ref, ys):
    acc = p_ref[:, 0:1] * _load_rows(ys[0])
    for k in range(1, TOP_K):
        acc = acc + p_ref[:, k:k + 1] * _load_rows(ys[k])
    return acc


def _combine_body(x_ref, g_ref, p_ref, y0, y1, y2, y3, o_ref):
    o_ref[...] = x_ref[...] + g_ref[...] * _weighted(p_ref, (y0, y1, y2, y3))


def _combine_norm_body(x_ref, g_ref, p_ref, y0, y1, y2, y3, nw_ref, o_ref):
    x = x_ref[...] + g_ref[...] * _weighted(p_ref, (y0, y1, y2, y3))
    o_ref[...] = x * lax.rsqrt(jnp.mean(x * x, axis=-1, keepdims=True) + EPS) * nw_ref[...]


def _combine_nm_body(x_ref, g_ref, p_ref, y0, y1, y2, y3, nw_ref, sc_ref, sh_ref, w_ref, x2_ref, o_ref):
    x2 = x_ref[...] + g_ref[...] * _weighted(p_ref, (y0, y1, y2, y3))
    x2_ref[...] = x2
    h = _normmod(x2, nw_ref[...], sc_ref[...], sh_ref[...]).astype(BF16)
    for s, n in _col_chunks(o_ref.shape[1]):
        o_ref[:, s:s + n] = jnp.dot(h, w_ref[:, s:s + n], preferred_element_type=F32)


def _combine_inproj(x, mod, layer, prr, y_all, tok0, tm, nw, w_bf, name):
    t = x.shape[0]
    n = w_bf.shape[1]
    rows = mod.shape[2]
    tps = (t // mod.shape[1]) // tm
    yspec = lambda k: pl.BlockSpec((tm * ROW_SUB, LANES), lambda i: ((k * T_PAD + tok0) // tm + i, 0))
    return pl.pallas_call(
        _combine_nm_body,
        grid=(t // tm,),
        in_specs=[pl.BlockSpec((tm, D_MODEL), lambda i: (i, 0)), _mod_spec(rows, layer, 5, tps),
                  pl.BlockSpec((tm, LANES), lambda i: (i, 0)), yspec(0), yspec(1), yspec(2), yspec(3),
                  _resident((1, D_MODEL)), _mod_spec(rows, layer + 1, 1, tps), _mod_spec(rows, layer + 1, 0, tps),
                  _resident(w_bf.shape)],
        out_specs=[pl.BlockSpec((tm, D_MODEL), lambda i: (i, 0)), pl.BlockSpec((tm, n), lambda i: (i, 0))],
        out_shape=[jax.ShapeDtypeStruct((t, D_MODEL), F32), jax.ShapeDtypeStruct((t, n), F32)],
        compiler_params=_cp(("arbitrary",)),
        name=name,
    )(x, mod, prr, y_all, y_all, y_all, y_all, nw, mod, mod, w_bf)


def _combine(x, mod, layer, prr, y_all, tok0, tm, final_w=None, name="combine"):
    t = x.shape[0]
    rows = mod.shape[2]
    tps = (t // mod.shape[1]) // tm
    yspec = lambda k: pl.BlockSpec((tm * ROW_SUB, LANES), lambda i: ((k * T_PAD + tok0) // tm + i, 0))
    in_specs = [pl.BlockSpec((tm, D_MODEL), lambda i: (i, 0)), _mod_spec(rows, layer, 5, tps),
                pl.BlockSpec((tm, LANES), lambda i: (i, 0)), yspec(0), yspec(1), yspec(2), yspec(3)]
    args = [x, mod, prr, y_all, y_all, y_all, y_all]
    body = _combine_body
    if final_w is not None:
        in_specs.append(_resident((1, D_MODEL)))
        args.append(final_w)
        body = _combine_norm_body
    return pl.pallas_call(
        body,
        grid=(t // tm,),
        in_specs=in_specs,
        out_specs=pl.BlockSpec((tm, D_MODEL), lambda i: (i, 0)),
        out_shape=jax.ShapeDtypeStruct((t, D_MODEL), F32),
        compiler_params=_cp(("arbitrary",)),
        name=name,
    )(*args)


def _mixer_out_and_moe(layer, x_p, a_p, x_s, a_s, w_out, mod_p, mod_s, nw, rwt, rb, tm_p, tm_s,
                       moe_w_gate_up, bgu_perm, moe_w_down, moe_b_down, perm):
    zero_cnt = jnp.zeros((N_EXPERTS, 1), F32)
    x1_p, h_all, idx_p, rank_p, prr_p, cnt = _outproj(x_p, a_p, w_out, mod_p, layer, nw, rwt, rb, tm_p, zero_cnt,
                                                      None, "outproj%d_prompt" % layer)
    x1_s, h_all, idx_s, rank_s, prr_s, cnt = _outproj(x_s, a_s, w_out, mod_s, layer, nw, rwt, rb, tm_s, cnt,
                                                      h_all, "outproj%d_sample" % layer)
    idx_t = jnp.concatenate([idx_p, idx_s], axis=1)
    rank_t = jnp.concatenate([rank_p, rank_s], axis=1)
    te, n_active, row_tok, row_dst = _route(idx_t, rank_t, cnt)
    y_all = _moe_call(layer, h_all, te, n_active, row_tok, row_dst, moe_w_gate_up, bgu_perm,
                      moe_w_down, moe_b_down.reshape(-1, N_EXPERTS, 1, D_MODEL), perm)
    return x1_p, x1_s, prr_p, prr_s, y_all


def kernel(x_prompt, x_sample, c_prompt, c_sample, state_ssm, state_ssm_conv, state_sconv, ada_w, ada_b, norm1_w, norm2_w, ssd_w_in, ssd_conv_w, ssd_conv_b, ssd_dt_bias, ssd_A_log, ssd_D, ssd_norm_w, ssd_w_out, sc_w_in, sc_conv_w, sc_w_out, router_w, router_b, moe_w_gate_up, moe_b_gate_up, moe_w_down, moe_b_down, final_norm_w):
    tm_p = 256
    tm_s = DEC_BATCH
    depth = ada_w.shape[0]

    def hilo(w):
        hi = w.astype(BF16)
        return jnp.stack([hi, (w - hi.astype(F32)).astype(BF16)])

    w_in = hilo(ssd_w_in)
    w_in0 = jnp.concatenate(
        [w_in[..., SSD_INNER:SSD_INNER + SSD_CONV_DIM], w_in[..., :SSD_INNER],
         w_in[..., SSD_INNER + SSD_CONV_DIM:], jnp.zeros((2, D_MODEL, HEADS_PAD - SSD_HEADS), BF16)],
        axis=-1)
    w_out0 = hilo(ssd_w_out)
    w_in1 = sc_w_in.astype(BF16)
    w_out1 = sc_w_out.astype(BF16)
    pad_h = lambda v: jnp.concatenate([v, jnp.zeros((HEADS_PAD - SSD_HEADS,), F32)]).reshape(1, HEADS_PAD)
    dtb = pad_h(ssd_dt_bias)
    alog = pad_h(ssd_A_log)
    dch = jnp.repeat(ssd_D, SSD_HEAD_DIM).reshape(1, SSD_INNER)
    ssd_nw = ssd_norm_w.reshape(1, SSD_INNER)
    conv_b = ssd_conv_b.reshape(1, SSD_CONV_DIM)
    emat = (jnp.arange(SSD_INNER, dtype=jnp.int32)[None, :] // SSD_HEAD_DIM
            == jnp.arange(HEADS_PAD, dtype=jnp.int32)[:, None]).astype(BF16)
    jj = jnp.arange(256, dtype=jnp.int32)
    perm = (jj[None, :] == jnp.where(jj % 2 == 0, jj // 2, 128 + jj // 2)[:, None]).astype(BF16)
    bgu_perm = moe_b_gate_up.reshape(depth, N_EXPERTS, 8, 128, 2).transpose(0, 1, 2, 4, 3).reshape(
        depth, N_EXPERTS, MOE_NBLK, 256)
    rwt = jnp.transpose(router_w, (0, 2, 1))
    rb = router_b.reshape(depth, N_EXPERTS, 1)
    n1 = norm1_w.reshape(depth, 1, D_MODEL)
    n2 = norm2_w.reshape(depth, 1, D_MODEL)

    mod = _ada_call(jnp.concatenate([c_prompt, c_sample], axis=0), ada_w, ada_b)
    mod_p = mod[:, :BATCH].reshape(depth, BATCH, 1, 6 * D_MODEL)
    mod_s = mod[:, BATCH:].reshape(depth, 1, DEC_BATCH, 6 * D_MODEL)

    xp = x_prompt.reshape(T_PROMPT, D_MODEL)
    xs = x_sample.reshape(DEC_BATCH, D_MODEL)

    zx_p = _normmod_matmul(xp, n1[0], mod_p, 0, 1, 0, w_in0, tm_p, "inproj0_prompt")
    gn_p, ssm_p, ssm_conv_p = _ssd_prompt(zx_p, ssd_conv_w, conv_b, dtb, alog, dch, ssd_nw, emat)
    zx_s = _normmod_matmul(xs, n1[0], mod_s, 0, 1, 0, w_in0, tm_s, "inproj0_sample")
    xc_s, xdt_s, dec_s, ncs_s = _ssd_sample_prep(
        zx_s, state_ssm_conv.reshape(DEC_BATCH, (SSD_CONV - 1) * SSD_CONV_DIM), ssd_conv_w, conv_b, dtb, alog, emat)
    steps = DEC_BATCH // SAMPLE_BS
    to_cols = lambda a: a.reshape(steps, SAMPLE_BS, SSD_INNER).transpose(0, 2, 1)
    b3 = xc_s[:, SSD_INNER:SSD_INNER + SSD_GN].reshape(DEC_BATCH, SSD_GROUPS, SSD_STATE)
    c3 = xc_s[:, SSD_INNER + SSD_GN:].reshape(DEC_BATCH, SSD_GROUPS, SSD_STATE).transpose(1, 0, 2)
    ssm_s, gn_s = _ssd_sample_state(state_ssm, to_cols(xdt_s), dec_s, b3, c3, xc_s, zx_s, dch, ssd_nw)
    ssm_conv_s = ncs_s.reshape(DEC_BATCH, SSD_CONV - 1, SSD_CONV_DIM)

    x1_p, x1_s, prr_p, prr_s, y0 = _mixer_out_and_moe(
        0, xp, gn_p, xs, gn_s, w_out0, mod_p, mod_s, n2[0], rwt[0], rb[0], tm_p, tm_s,
        moe_w_gate_up, bgu_perm, moe_w_down, moe_b_down, perm)
    x2_p, bcx_p = _combine_inproj(x1_p, mod_p, 0, prr_p, y0, 0, tm_p, n1[1], w_in1, "combine0_inproj1_prompt")
    x2_s, bcx_s = _combine_inproj(x1_s, mod_s, 0, prr_s, y0, T_PROMPT, tm_s, n1[1], w_in1,
                                  "combine0_inproj1_sample")
    v_p, sconv_p = _sconv_prompt(bcx_p, sc_conv_w)
    v_s, nsc_s = _sconv_sample(bcx_s, state_sconv.reshape(DEC_BATCH, (SC_WIDTH - 1) * D_MODEL), sc_conv_w)
    sconv_s = nsc_s.reshape(DEC_BATCH, SC_WIDTH - 1, D_MODEL)

    x3_p, x3_s, prr_p, prr_s, y1 = _mixer_out_and_moe(
        1, x2_p, v_p, x2_s, v_s, w_out1, mod_p, mod_s, n2[1], rwt[1], rb[1], tm_p, tm_s,
        moe_w_gate_up, bgu_perm, moe_w_down, moe_b_down, perm)
    fw = final_norm_w.reshape(1, D_MODEL)
    y_p = _combine(x3_p, mod_p, 1, prr_p, y1, 0, tm_p, final_w=fw, name="final_prompt")
    y_s = _combine(x3_s, mod_s, 1, prr_s, y1, T_PROMPT, tm_s, final_w=fw, name="final_sample")

    return (y_p.reshape(BATCH, SEQ, D_MODEL), y_s.reshape(DEC_BATCH, 1, D_MODEL), ssm_p, ssm_conv_p, sconv_p,
            ssm_s, ssm_conv_s, sconv_s)
```

```python
import functools

import jax
import jax.numpy as jnp
from jax import lax
from jax.experimental import pallas as pl
from jax.experimental.pallas import tpu as pltpu

F32 = jnp.float32
BF16 = jnp.bfloat16

D_MODEL = 1024
BATCH = 8
SEQ = 2048
DEC_BATCH = 128
SSD_INNER = 2048
SSD_HEAD_DIM = 64
SSD_HEADS = 32
SSD_GROUPS = 8
SSD_HPG = 4
SSD_STATE = 128
SSD_CONV = 4
SSD_CHUNK = 128
SSD_GN = 1024
SSD_CONV_DIM = 4096
SC_WIDTH = 3
N_EXPERTS = 32
TOP_K = 4
SWIGLU_LIMIT = 7.0
SWIGLU_ALPHA = 1.702
EPS = 1e-5

LANES = 128
HEADS_PAD = LANES
ZX_COLS = SSD_CONV_DIM + SSD_INNER + HEADS_PAD
T_PROMPT = BATCH * SEQ
T_ALL = T_PROMPT + DEC_BATCH
MOE_TM = 256
T_PAD = 16640
N_ASSIGN = TOP_K * T_ALL
MOE_TILES = (N_ASSIGN + N_EXPERTS * (MOE_TM - 1)) // MOE_TM + 1
P_PAD = MOE_TILES * MOE_TM
Y_ROWS = TOP_K * T_PAD + 2 * MOE_TM
VMEM_LIMIT = 48 * 1024 * 1024


def _cp(sem, vmem=VMEM_LIMIT):
    return pltpu.CompilerParams(dimension_semantics=sem, vmem_limit_bytes=vmem)


def _sigmoid(x):
    return 1.0 / (1.0 + jnp.exp(-x))


def _silu(x):
    return x * _sigmoid(x)


def _softplus(x):
    return jnp.maximum(x, 0.0) + jnp.log1p(jnp.exp(-jnp.abs(x)))


def _normmod(x, nw, sc, sh):
    y = x * lax.rsqrt(jnp.mean(x * x, axis=-1, keepdims=True) + EPS)
    return y * nw * (1.0 + sc) + sh


def _split3(x):
    hi = x.astype(BF16)
    r1 = x - hi.astype(F32)
    mid = r1.astype(BF16)
    lo = (r1 - mid.astype(F32)).astype(BF16)
    return hi, mid, lo


def _dot_exact_lhs(x, m01):
    hi, mid, lo = _split3(x)
    d = functools.partial(jnp.dot, preferred_element_type=F32)
    return d(hi, m01) + d(mid, m01) + d(lo, m01)


def _dot_exact_rhs(m01, x):
    hi, mid, lo = _split3(x)
    d = functools.partial(jnp.dot, preferred_element_type=F32)
    return d(m01, hi) + d(m01, mid) + d(m01, lo)


def _dot_nt(a, b):
    return lax.dot_general(a, b, (((1,), (1,)), ((), ())), preferred_element_type=F32)


def _split2(x):
    hi = x.astype(BF16)
    return hi, (x - hi.astype(F32)).astype(BF16)


def _dot_x3(a, b, nt=False):
    d = _dot_nt if nt else functools.partial(jnp.dot, preferred_element_type=F32)
    return d(a[0], b[0]) + d(a[1], b[0]) + d(a[0], b[1])


ROW_SUB = D_MODEL // LANES


def _load_rows(ref):
    tm = ref.shape[0] // ROW_SUB
    return jnp.concatenate([ref[pl.ds(c, tm, stride=ROW_SUB), :] for c in range(ROW_SUB)], axis=1)


def _store_rows(ref, x):
    tm = ref.shape[0] // ROW_SUB
    for c in range(ROW_SUB):
        ref[pl.ds(c, tm, stride=ROW_SUB), :] = x[:, c * LANES:(c + 1) * LANES]


def _ada_body(c_ref, w_ref, b_ref, o_ref):
    o_ref[...] = _dot_x3(_split2(_silu(c_ref[...])), _split2(w_ref[...])) + b_ref[...]


def _ada_call(c_all, ada_w, ada_b):
    n = c_all.shape[0]
    depth = ada_w.shape[0]
    return pl.pallas_call(
        _ada_body,
        grid=(depth, 6),
        in_specs=[
            pl.BlockSpec((n, D_MODEL), lambda l, j: (0, 0)),
            pl.BlockSpec((None, D_MODEL, D_MODEL), lambda l, j: (l, 0, j)),
            pl.BlockSpec((None, 1, D_MODEL), lambda l, j: (l, 0, j)),
        ],
        out_specs=pl.BlockSpec((None, n, D_MODEL), lambda l, j: (l, 0, j)),
        out_shape=jax.ShapeDtypeStruct((depth, n, 6 * D_MODEL), F32),
        compiler_params=_cp(("arbitrary", "arbitrary")),
        name="ada_mod",
    )(c_all, ada_w, ada_b.reshape(depth, 1, 6 * D_MODEL))


def _col_chunks(n, step=512):
    return [(s, min(step, n - s)) for s in range(0, n, step)]


def _rows_x3(a_hi, a_hilo, w_ref, s, n):
    tm = a_hi.shape[0]
    r = jnp.dot(a_hilo, w_ref[0, :, s:s + n], preferred_element_type=F32)
    return r[:tm] + r[tm:] + jnp.dot(a_hi, w_ref[1, :, s:s + n], preferred_element_type=F32)


def _nm_body(x_ref, nw_ref, sc_ref, sh_ref, w_ref, o_ref):
    h = _normmod(x_ref[...], nw_ref[...], sc_ref[...], sh_ref[...])
    if len(w_ref.shape) == 3:
        h_hi, h_lo = _split2(h)
        h_hilo = jnp.concatenate([h_hi, h_lo], axis=0)
        for src, dst, width in ((SSD_INNER, 0, SSD_CONV_DIM), (0, SSD_CONV_DIM, SSD_INNER)):
            for s, n in _col_chunks(width):
                o_ref[:, dst + s:dst + s + n] = _rows_x3(h_hi, h_hilo, w_ref, src + s, n)
        dt0 = SSD_CONV_DIM + SSD_INNER
        o_ref[:, dt0:dt0 + SSD_HEADS] = _rows_x3(h_hi, h_hilo, w_ref, dt0, SSD_HEADS)
        o_ref[:, dt0 + SSD_HEADS:] = jnp.zeros((h.shape[0], HEADS_PAD - SSD_HEADS), F32)
    else:
        h = h.astype(BF16)
        for s, n in _col_chunks(o_ref.shape[1]):
            o_ref[:, s:s + n] = jnp.dot(h, w_ref[:, s:s + n], preferred_element_type=F32)


def _mod_spec(rows, layer, col, tiles_per_seq):
    return pl.BlockSpec((None, None, rows, D_MODEL), lambda i: (layer, i // tiles_per_seq, 0, col))


def _resident(shape):
    return pl.BlockSpec(shape, lambda *_: tuple(0 for _ in shape), pipeline_mode=pl.Buffered(1))


def _normmod_matmul(x, nw, mod, layer, sc_col, sh_col, w_bf, tm, name):
    t = x.shape[0]
    n = ZX_COLS if w_bf.ndim == 3 else w_bf.shape[-1]
    rows = mod.shape[2]
    tps = (t // mod.shape[1]) // tm
    return pl.pallas_call(
        _nm_body,
        grid=(t // tm,),
        in_specs=[
            pl.BlockSpec((tm, D_MODEL), lambda i: (i, 0)),
            _resident((1, D_MODEL)),
            _mod_spec(rows, layer, sc_col, tps),
            _mod_spec(rows, layer, sh_col, tps),
            _resident(w_bf.shape),
        ],
        out_specs=pl.BlockSpec((tm, n), lambda i: (i, 0)),
        out_shape=jax.ShapeDtypeStruct((t, n), F32),
        compiler_params=_cp(("arbitrary",)),
        name=name,
    )(x, nw, mod, mod, w_bf)


def _ssd_prompt_body(xbc_ref, z_ref, dt_ref, cw_ref, cb_ref, dtb_ref, alog_ref, dch_ref, nw_ref, e_ref,
                     gn_ref, st_ref, cs_ref, cbuf, ht, xc_s, yd_s):
    q = SSD_CHUNK
    c = pl.program_id(1)

    @pl.when(c == 0)
    def _():
        cbuf[0:8, :] = jnp.zeros((8, SSD_CONV_DIM), F32)
        ht[...] = jnp.zeros_like(ht)

    cbuf[8:8 + q, :] = xbc_ref[...]
    for s, n in _col_chunks(SSD_CONV_DIM):
        acc = cbuf[5:5 + q, s:s + n] * cw_ref[0:1, s:s + n]
        for k in range(1, SSD_CONV):
            acc = acc + cbuf[5 + k:5 + k + q, s:s + n] * cw_ref[k:k + 1, s:s + n]
        xc_s[:, s:s + n] = _silu(acc + cb_ref[:, s:s + n])
    cs_ref[...] = cbuf[5 + q:8 + q, :]
    cbuf[0:8, :] = cbuf[q:q + 8, :]

    dt = _softplus(dt_ref[...] + dtb_ref[...])
    da = dt * (-jnp.exp(alog_ref[...]))
    ri = lax.broadcasted_iota(jnp.int32, (q, q), 0)
    ci = lax.broadcasted_iota(jnp.int32, (q, q), 1)
    causal = ri >= ci
    tri = jnp.where(causal, 1.0, 0.0).astype(BF16)
    tri_t = jnp.where(ri <= ci, 1.0, 0.0).astype(BF16)
    acs = _dot_exact_rhs(tri, da)
    acs_t = _dot_exact_lhs(da.T, tri_t)

    for g in range(SSD_GROUPS):
        cs = slice(g * 256, (g + 1) * 256)
        e_g = e_ref[:, cs]
        dte = _dot_exact_lhs(dt, e_g)
        ace = _dot_exact_lhs(acs, e_g)
        last = ace[q - 1:q, :]
        xs_g = xc_s[:, cs]
        xdt = xs_g * dte
        b_g = xc_s[:, SSD_INNER + g * SSD_STATE:SSD_INNER + (g + 1) * SSD_STATE]
        c_g = xc_s[:, SSD_INNER + SSD_GN + g * SSD_STATE:SSD_INNER + SSD_GN + (g + 1) * SSD_STATE]
        c_2 = _split2(c_g)
        cb = _dot_x3(c_2, _split2(b_g), nt=True)
        h_prev = ht[g]
        y_off = _dot_x3(c_2, _split2(h_prev)) * jnp.exp(ace)
        st_t = _dot_x3(_split2(b_g.T), _split2(jnp.exp(last - ace) * xdt))
        ht[g] = jnp.exp(last) * h_prev + st_t
        xdt_hi, xdt_lo = _split2(xdt)
        for r in range(SSD_HPG):
            h = g * SSD_HPG + r
            hs = slice(r * 64, (r + 1) * 64)
            diff = acs[:, h:h + 1] - acs_t[h:h + 1, :]
            m_2 = _split2(cb * jnp.where(causal, jnp.exp(diff), 0.0))
            yd_s[:, hs] = _dot_x3(m_2, (xdt_hi[:, hs], xdt_lo[:, hs]))
        y = yd_s[...] + y_off + xs_g * dch_ref[:, cs]
        gg = y * _silu(z_ref[:, cs])
        ms = jnp.mean(gg * gg, axis=-1, keepdims=True)
        gn_ref[:, cs] = gg * lax.rsqrt(ms + EPS) * nw_ref[:, cs]

    @pl.when(c == pl.num_programs(1) - 1)
    def _():
        for g in range(SSD_GROUPS):
            t = ht[g].T
            for r in range(SSD_HPG):
                st_ref[g * SSD_HPG + r] = t[r * 64:(r + 1) * 64, :]


def _ssd_prompt(zx, conv_w, conv_b, dtb, alog, dch, nw, emat):
    nc = SEQ // SSD_CHUNK
    q = SSD_CHUNK
    row = lambda b, c: b * nc + c
    return pl.pallas_call(
        _ssd_prompt_body,
        grid=(BATCH, nc),
        in_specs=[
            pl.BlockSpec((q, SSD_CONV_DIM), lambda b, c: (row(b, c), 0)),
            pl.BlockSpec((q, SSD_INNER), lambda b, c: (row(b, c), 2)),
            pl.BlockSpec((q, HEADS_PAD), lambda b, c: (row(b, c), (SSD_CONV_DIM + SSD_INNER) // HEADS_PAD)),
            _resident((SSD_CONV, SSD_CONV_DIM)),
            _resident((1, SSD_CONV_DIM)),
            _resident((1, HEADS_PAD)),
            _resident((1, HEADS_PAD)),
            _resident((1, SSD_INNER)),
            _resident((1, SSD_INNER)),
            _resident((HEADS_PAD, SSD_INNER)),
        ],
        out_specs=[
            pl.BlockSpec((q, SSD_INNER), lambda b, c: (row(b, c), 0)),
            pl.BlockSpec((None, SSD_HEADS, SSD_HEAD_DIM, SSD_STATE), lambda b, c: (b, 0, 0, 0)),
            pl.BlockSpec((None, SSD_CONV - 1, SSD_CONV_DIM), lambda b, c: (b, 0, 0)),
        ],
        out_shape=[
            jax.ShapeDtypeStruct((T_PROMPT, SSD_INNER), F32),
            jax.ShapeDtypeStruct((BATCH, SSD_HEADS, SSD_HEAD_DIM, SSD_STATE), F32),
            jax.ShapeDtypeStruct((BATCH, SSD_CONV - 1, SSD_CONV_DIM), F32),
        ],
        scratch_shapes=[
            pltpu.VMEM((8 + q, SSD_CONV_DIM), F32),
            pltpu.VMEM((SSD_GROUPS, SSD_STATE, 256), F32),
            pltpu.VMEM((q, SSD_CONV_DIM), F32),
            pltpu.VMEM((q, 256), F32),
        ],
        compiler_params=_cp(("arbitrary", "arbitrary")),
        name="ssd_prompt",
    )(zx, zx, zx, conv_w, conv_b, dtb, alog, dch, nw, emat)


def _ssd_sample_prep_body(zx_ref, cst_ref, cw_ref, cb_ref, dtb_ref, alog_ref, e_ref,
                          xc_ref, xdt_ref, dec_ref, ncs_ref):
    kc = SSD_CONV_DIM
    for s, n in _col_chunks(kc):
        xnew = zx_ref[:, s:s + n]
        acc = xnew * cw_ref[SSD_CONV - 1:SSD_CONV, s:s + n]
        for k in range(SSD_CONV - 1):
            acc = acc + cst_ref[:, k * kc + s:k * kc + s + n] * cw_ref[k:k + 1, s:s + n]
        xc_ref[:, s:s + n] = _silu(acc + cb_ref[:, s:s + n])
        for k in range(SSD_CONV - 2):
            ncs_ref[:, k * kc + s:k * kc + s + n] = cst_ref[:, (k + 1) * kc + s:(k + 1) * kc + s + n]
        ncs_ref[:, (SSD_CONV - 2) * kc + s:(SSD_CONV - 2) * kc + s + n] = xnew
    dt = _softplus(zx_ref[:, SSD_CONV_DIM + SSD_INNER:] + dtb_ref[...])
    dec_ref[...] = jnp.exp(dt * (-jnp.exp(alog_ref[...])))
    for s, n in _col_chunks(SSD_INNER):
        xdt_ref[:, s:s + n] = xc_ref[:, s:s + n] * _dot_exact_lhs(dt, e_ref[:, s:s + n])


def _ssd_sample_prep(zx, cst, conv_w, conv_b, dtb, alog, emat):
    n = DEC_BATCH
    full = lambda shape: pl.BlockSpec(shape, lambda i: tuple(0 for _ in shape))
    return pl.pallas_call(
        _ssd_sample_prep_body,
        grid=(1,),
        in_specs=[full((n, ZX_COLS)), full((n, (SSD_CONV - 1) * SSD_CONV_DIM)), full((SSD_CONV, SSD_CONV_DIM)),
                  full((1, SSD_CONV_DIM)), full((1, HEADS_PAD)), full((1, HEADS_PAD)),
                  full((HEADS_PAD, SSD_INNER))],
        out_specs=[full((n, SSD_CONV_DIM)), full((n, SSD_INNER)), full((n, HEADS_PAD)),
                   full((n, (SSD_CONV - 1) * SSD_CONV_DIM))],
        out_shape=[
            jax.ShapeDtypeStruct((n, SSD_CONV_DIM), F32),
            jax.ShapeDtypeStruct((n, SSD_INNER), F32),
            jax.ShapeDtypeStruct((n, HEADS_PAD), F32),
            jax.ShapeDtypeStruct((n, (SSD_CONV - 1) * SSD_CONV_DIM), F32),
        ],
        compiler_params=_cp(("arbitrary",)),
        name="ssd_sample_prep",
    )(zx, cst, conv_w, conv_b, dtb, alog, emat)


SAMPLE_BS = 8


def _ssd_sample_state_body(dec_ref, st_ref, xt_ref, b_ref, c_ref, xs_ref, z_ref, dch_ref, nw_ref,
                           so_ref, gn_ref, y_s):
    def group(g, carry):
        r0 = pl.multiple_of(g * 256, 256)
        c_2 = _split2(c_ref[g])
        for i in range(SAMPLE_BS):
            h = st_ref[i, pl.ds(g * SSD_HPG, SSD_HPG)]
            hd = jnp.concatenate([h[r] * dec_ref[i, g * SSD_HPG + r] for r in range(SSD_HPG)], axis=0)
            xcol = xt_ref[pl.ds(r0, 256), i:i + 1]
            brow = b_ref[i, pl.ds(g, 1), :]
            hn = hd + xcol * brow
            so_ref[i, pl.ds(g * SSD_HPG, SSD_HPG)] = hn.reshape(SSD_HPG, SSD_HEAD_DIM, SSD_STATE)
            yall = _dot_x3(c_2, _split2(hn), nt=True)
            y_s[g, i:i + 1, :] = yall[i:i + 1, :]
        return carry
    lax.fori_loop(0, SSD_GROUPS, group, 0)
    for g in range(SSD_GROUPS):
        cs = slice(g * 256, (g + 1) * 256)
        y = y_s[g] + xs_ref[:, cs] * dch_ref[:, cs]
        gg = y * _silu(z_ref[:, cs])
        ms = jnp.mean(gg * gg, axis=-1, keepdims=True)
        gn_ref[:, cs] = gg * lax.rsqrt(ms + EPS) * nw_ref[:, cs]


def _ssd_sample_state(state, xt, dec, b3, c3, xc, zx, dch, nw):
    bs = SAMPLE_BS
    steps = DEC_BATCH // bs
    return pl.pallas_call(
        _ssd_sample_state_body,
        grid=(steps,),
        in_specs=[
            pl.BlockSpec((None, bs, SSD_HEADS), lambda s: (s, 0, 0), memory_space=pltpu.SMEM),
            pl.BlockSpec((bs, SSD_HEADS, SSD_HEAD_DIM, SSD_STATE), lambda s: (s, 0, 0, 0)),
            pl.BlockSpec((None, SSD_INNER, bs), lambda s: (s, 0, 0)),
            pl.BlockSpec((bs, SSD_GROUPS, SSD_STATE), lambda s: (s, 0, 0)),
            pl.BlockSpec((SSD_GROUPS, bs, SSD_STATE), lambda s: (0, s, 0)),
            pl.BlockSpec((bs, SSD_INNER), lambda s: (s, 0)),
            pl.BlockSpec((bs, SSD_INNER), lambda s: (s, 2)),
            _resident((1, SSD_INNER)),
            _resident((1, SSD_INNER)),
        ],
        out_specs=[
            pl.BlockSpec((bs, SSD_HEADS, SSD_HEAD_DIM, SSD_STATE), lambda s: (s, 0, 0, 0)),
            pl.BlockSpec((bs, SSD_INNER), lambda s: (s, 0)),
        ],
        out_shape=[
            jax.ShapeDtypeStruct(state.shape, F32),
            jax.ShapeDtypeStruct((DEC_BATCH, SSD_INNER), F32),
        ],
        scratch_shapes=[pltpu.VMEM((SSD_GROUPS, bs, 256), F32)],
        compiler_params=_cp(("arbitrary",), 56 * 1024 * 1024),
        name="ssd_sample_state",
    )(dec[:, :SSD_HEADS].reshape(steps, bs, SSD_HEADS), state, xt, b3, c3, xc, zx, dch, nw)


def _sconv_prompt_body(bg_ref, cg_ref, xh_ref, cw_ref, v_ref, st_ref, cbuf):
    tm = bg_ref.shape[0]
    j = pl.program_id(1)

    @pl.when(j == 0)
    def _():
        cbuf[0:8, :] = jnp.zeros((8, D_MODEL), F32)

    cbuf[8:8 + tm, :] = cg_ref[...] * xh_ref[...]
    acc = cbuf[6:6 + tm, :] * cw_ref[0:1, :]
    for k in range(1, SC_WIDTH):
        acc = acc + cbuf[6 + k:6 + k + tm, :] * cw_ref[k:k + 1, :]
    v_ref[...] = (bg_ref[...] * acc).astype(BF16)
    st_ref[...] = cbuf[6 + tm:8 + tm, :]
    cbuf[0:8, :] = cbuf[tm:tm + 8, :]


def _sconv_prompt(bcx, conv_w, tm=256):
    nt = SEQ // tm
    row = lambda b, j: b * nt + j
    return pl.pallas_call(
        _sconv_prompt_body,
        grid=(BATCH, nt),
        in_specs=[
            pl.BlockSpec((tm, D_MODEL), lambda b, j: (row(b, j), 0)),
            pl.BlockSpec((tm, D_MODEL), lambda b, j: (row(b, j), 1)),
            pl.BlockSpec((tm, D_MODEL), lambda b, j: (row(b, j), 2)),
            _resident((SC_WIDTH, D_MODEL)),
        ],
        out_specs=[
            pl.BlockSpec((tm, D_MODEL), lambda b, j: (row(b, j), 0)),
            pl.BlockSpec((None, SC_WIDTH - 1, D_MODEL), lambda b, j: (b, 0, 0)),
        ],
        out_shape=[
            jax.ShapeDtypeStruct((T_PROMPT, D_MODEL), BF16),
            jax.ShapeDtypeStruct((BATCH, SC_WIDTH - 1, D_MODEL), F32),
        ],
        scratch_shapes=[pltpu.VMEM((8 + tm, D_MODEL), F32)],
        compiler_params=_cp(("arbitrary", "arbitrary")),
        name="sconv_prompt",
    )(bcx, bcx, bcx, conv_w)


def _sconv_sample_body(bcx_ref, st_ref, cw_ref, v_ref, ns_ref):
    d = D_MODEL
    u = bcx_ref[:, d:2 * d] * bcx_ref[:, 2 * d:3 * d]
    acc = u * cw_ref[SC_WIDTH - 1:SC_WIDTH, :]
    for k in range(SC_WIDTH - 1):
        acc = acc + st_ref[:, k * d:(k + 1) * d] * cw_ref[k:k + 1, :]
    v_ref[...] = (bcx_ref[:, 0:d] * acc).astype(BF16)
    for k in range(SC_WIDTH - 2):
        ns_ref[:, k * d:(k + 1) * d] = st_ref[:, (k + 1) * d:(k + 2) * d]
    ns_ref[:, (SC_WIDTH - 2) * d:] = u


def _sconv_sample(bcx, st, conv_w):
    n = DEC_BATCH
    full = lambda shape: pl.BlockSpec(shape, lambda i: tuple(0 for _ in shape))
    return pl.pallas_call(
        _sconv_sample_body,
        grid=(1,),
        in_specs=[full((n, 3 * D_MODEL)), full((n, (SC_WIDTH - 1) * D_MODEL)), full((SC_WIDTH, D_MODEL))],
        out_specs=[full((n, D_MODEL)), full((n, (SC_WIDTH - 1) * D_MODEL))],
        out_shape=[jax.ShapeDtypeStruct((n, D_MODEL), BF16),
                   jax.ShapeDtypeStruct((n, (SC_WIDTH - 1) * D_MODEL), F32)],
        compiler_params=_cp(("arbitrary",)),
        name="sconv_sample",
    )(bcx, st, conv_w)


def _outproj_body(aliased, x_ref, a_ref, w_ref, g1_ref, nw_ref, sc_ref, sh_ref, rwt_ref, rb_ref, u_ref, cin_ref,
                  *refs):
    x1_ref, h2_ref, idx_ref, rank_ref, prr_ref, cnt_ref, run = refs[1:] if aliased else refs

    @pl.when(pl.program_id(0) == 0)
    def _():
        run[...] = cin_ref[...]

    if len(w_ref.shape) == 3:
        a_hi, a_lo = _split2(a_ref[...])
        m = _rows_x3(a_hi, jnp.concatenate([a_hi, a_lo], axis=0), w_ref, 0, D_MODEL)
    else:
        m = jnp.dot(a_ref[...], w_ref[...], preferred_element_type=F32)
    x1 = x_ref[...] + g1_ref[...] * m
    x1_ref[...] = x1
    h2 = _normmod(x1, nw_ref[...], sc_ref[...], sh_ref[...])
    _store_rows(h2_ref, h2)
    h_hi = h2.astype(BF16)
    h_lo = (h2 - h_hi.astype(F32)).astype(BF16)
    rw = rwt_ref[...]
    w_hi = rw.astype(BF16)
    w_lo = (rw - w_hi.astype(F32)).astype(BF16)
    logits = _dot_nt(w_hi, h_hi) + _dot_nt(w_hi, h_lo) + _dot_nt(w_lo, h_hi) + rb_ref[...]
    eio = lax.broadcasted_iota(jnp.int32, logits.shape, 0)
    vals, idxs = [], []
    for _ in range(TOP_K):
        mx = jnp.max(logits, axis=0, keepdims=True)
        sel = jnp.min(jnp.where(logits == mx, eio, N_EXPERTS), axis=0, keepdims=True)
        vals.append(mx)
        idxs.append(sel)
        logits = jnp.where(eio == sel, -jnp.inf, logits)
    ex = [jnp.exp(v - vals[0]) for v in vals]
    tot = ex[0] + ex[1] + ex[2] + ex[3]
    idx_ref[...] = jnp.concatenate(idxs, axis=0)
    tm = logits.shape[1]
    base = run[...]
    ranks = []
    for k in range(TOP_K):
        ohf = jnp.where(eio == idxs[k], 1.0, 0.0)
        pref = jnp.dot(ohf.astype(BF16), u_ref[...], preferred_element_type=F32)
        ranks.append(jnp.sum(ohf * (pref - 1.0 + base), axis=0, keepdims=True))
        base = base + pref[:, tm - 1:tm]
    run[...] = base
    cnt_ref[...] = base
    rank_ref[...] = jnp.concatenate(ranks, axis=0).astype(jnp.int32)
    pr = jnp.concatenate([e / tot for e in ex] + [jnp.zeros((LANES - TOP_K, tm), F32)], axis=0)
    prr_ref[...] = pr.T


def _outproj(x, a, w_bf, mod, layer, nw, rwt, rb, tm, cnt_in, h_buf, name):
    t = x.shape[0]
    k = a.shape[1]
    rows = mod.shape[2]
    tps = (t // mod.shape[1]) // tm
    aliased = h_buf is not None
    tok0 = T_PROMPT if aliased else 0
    umat = (jnp.arange(tm, dtype=jnp.int32)[:, None] <= jnp.arange(tm, dtype=jnp.int32)[None, :]).astype(BF16)
    in_specs = [
        pl.BlockSpec((tm, D_MODEL), lambda i: (i, 0)),
        pl.BlockSpec((tm, k), lambda i: (i, 0)),
        _resident(w_bf.shape),
        _mod_spec(rows, layer, 2, tps),
        _resident((1, D_MODEL)),
        _mod_spec(rows, layer, 4, tps),
        _mod_spec(rows, layer, 3, tps),
        _resident((N_EXPERTS, D_MODEL)),
        _resident((N_EXPERTS, 1)),
        _resident((tm, tm)),
        _resident((N_EXPERTS, 1)),
    ]
    args = [x, a, w_bf, mod, nw, mod, mod, rwt, rb, umat, cnt_in]
    if aliased:
        in_specs.append(pl.BlockSpec(memory_space=pl.ANY))
        args.append(h_buf)
    return pl.pallas_call(
        functools.partial(_outproj_body, aliased),
        grid=(t // tm,),
        in_specs=in_specs,
        out_specs=[
            pl.BlockSpec((tm, D_MODEL), lambda i: (i, 0)),
            pl.BlockSpec((tm * ROW_SUB, LANES), lambda i: (tok0 // tm + i, 0)),
            pl.BlockSpec((TOP_K, tm), lambda i: (0, i)),
            pl.BlockSpec((TOP_K, tm), lambda i: (0, i)),
            pl.BlockSpec((tm, LANES), lambda i: (i, 0)),
            pl.BlockSpec((N_EXPERTS, 1), lambda i: (0, 0)),
        ],
        out_shape=[
            jax.ShapeDtypeStruct((t, D_MODEL), F32),
            jax.ShapeDtypeStruct((T_ALL * ROW_SUB, LANES), F32),
            jax.ShapeDtypeStruct((TOP_K, t), jnp.int32),
            jax.ShapeDtypeStruct((TOP_K, t), jnp.int32),
            jax.ShapeDtypeStruct((t, LANES), F32),
            jax.ShapeDtypeStruct((N_EXPERTS, 1), F32),
        ],
        scratch_shapes=[pltpu.VMEM((N_EXPERTS, 1), F32)],
        input_output_aliases={len(args) - 1: 1} if aliased else {},
        compiler_params=_cp(("arbitrary",)),
        name=name,
    )(*args)


MOE_NBLK = 8


def _moe_body(te_ref, na_ref, tok_ref, tokn_ref, dstp_ref, h_hbm, wgu_ref, bgu_ref, wd_ref, bd_ref,
              perm_ref, y_hbm, xbuf, ybuf, wgu_s, wd_s, act_s, gsem, ssem):
    i = pl.program_id(0)
    last = MOE_TILES - 1
    n_active = na_ref[0]
    slot = i % 2
    oslot = 1 - slot

    unroll = 16

    def issue_rows(start_row):
        def body(j, carry):
            for u in range(unroll):
                start_row(j * unroll + u)
            return carry
        lax.fori_loop(0, MOE_TM // unroll, body, 0)

    def hbm_row(ref, idx):
        return ref.at[pl.ds(pl.multiple_of(idx, ROW_SUB), ROW_SUB)]

    def vmem_row(buf, sl, r):
        return buf.at[sl, pl.ds(pl.multiple_of(r * ROW_SUB, ROW_SUB), ROW_SUB)]

    def gather_row(rows_ref, sl, r):
        pltpu.make_async_copy(hbm_row(h_hbm, rows_ref[0, r]), vmem_row(xbuf, sl, r), gsem.at[sl]).start(priority=0)

    def scatter_row(r):
        pltpu.make_async_copy(vmem_row(ybuf, oslot, r), hbm_row(y_hbm, dstp_ref[0, r]),
                              ssem.at[oslot]).start(priority=1)

    def gather_rows(rows_ref, sl):
        issue_rows(lambda r: gather_row(rows_ref, sl, r))

    def gather_and_scatter_rows():
        def both(r):
            gather_row(tokn_ref, oslot, r)
            scatter_row(r)
        issue_rows(both)

    def wait_rows(buf, sem, sl):
        pltpu.make_async_copy(buf.at[sl], buf.at[sl], sem.at[sl]).wait()

    @pl.when(i == 0)
    def _():
        gather_rows(tok_ref, 0)
        ybuf[...] = jnp.zeros_like(ybuf)
        pad = (T_PAD - T_ALL) * ROW_SUB
        fills = [pltpu.make_async_copy(ybuf.at[0, pl.ds(0, pad)],
                                       y_hbm.at[pl.ds((k * T_PAD + T_ALL) * ROW_SUB, pad)], ssem.at[0])
                 for k in range(TOP_K)]
        fills += [pltpu.make_async_copy(ybuf.at[0],
                                        y_hbm.at[pl.ds((TOP_K * T_PAD + d * MOE_TM) * ROW_SUB, MOE_TM * ROW_SUB)],
                                        ssem.at[0]) for d in range(2)]
        for f in fills:
            f.start()
        for f in fills:
            f.wait()

    @pl.when(i < last)
    def _():
        gather_and_scatter_rows()

    @pl.when(i == last)
    def _():
        issue_rows(scatter_row)

    wait_rows(xbuf, gsem, slot)

    @pl.when(i < n_active)
    def _():
        e_prev = te_ref[jnp.maximum(i - 1, 0)]

        @pl.when(jnp.logical_or(i == 0, e_prev != te_ref[i]))
        def _():
            for b in range(MOE_NBLK):
                wgu_s[b] = jnp.dot(wgu_ref[:, b * 256:(b + 1) * 256].astype(BF16), perm_ref[...],
                                   preferred_element_type=F32).astype(BF16)
            wd_s[...] = wd_ref[...].astype(BF16)

        x = _load_rows(xbuf.at[slot]).astype(BF16)
        for b in range(MOE_NBLK):
            gu = jnp.dot(x, wgu_s[b], preferred_element_type=F32) + bgu_ref[b:b + 1, :]
            gate = jnp.minimum(gu[:, :128], SWIGLU_LIMIT)
            up = jnp.clip(gu[:, 128:], -SWIGLU_LIMIT, SWIGLU_LIMIT)
            act_s[b] = ((up + 1.0) * (gate * _sigmoid(SWIGLU_ALPHA * gate))).astype(BF16)
        act = jnp.concatenate([act_s[b] for b in range(MOE_NBLK)], axis=1)
        y = jnp.dot(act, wd_s[...], preferred_element_type=F32) + bd_ref[...]

        @pl.when(i >= 1)
        def _():
            wait_rows(ybuf, ssem, slot)

        _store_rows(ybuf.at[slot], y)

    @pl.when(i >= n_active)
    def _():
        wait_rows(ybuf, ssem, slot)

        @pl.when(i == last)
        def _():
            wait_rows(ybuf, ssem, oslot)


def _moe_call(layer, h_all, tile_expert, n_active, row_tok, row_dst, wgu, bgu_perm, wd, bd, perm):
    smem_rows = lambda f: pl.BlockSpec((None, 1, MOE_TM), f, memory_space=pltpu.SMEM)
    grid_spec = pltpu.PrefetchScalarGridSpec(
        num_scalar_prefetch=2,
        grid=(MOE_TILES,),
        in_specs=[
            smem_rows(lambda i, te, na: (i, 0, 0)),
            smem_rows(lambda i, te, na: (jnp.minimum(i + 1, MOE_TILES - 1), 0, 0)),
            smem_rows(lambda i, te, na: (jnp.where(i == 0, MOE_TILES, i - 1), 0, 0)),
            pl.BlockSpec(memory_space=pl.ANY),
            pl.BlockSpec((None, None, D_MODEL, 2 * D_MODEL), lambda i, te, na: (layer, te[i], 0, 0)),
            pl.BlockSpec((None, None, MOE_NBLK, 256), lambda i, te, na: (layer, te[i], 0, 0)),
            pl.BlockSpec((None, None, D_MODEL, D_MODEL), lambda i, te, na: (layer, te[i], 0, 0)),
            pl.BlockSpec((None, None, 1, D_MODEL), lambda i, te, na: (layer, te[i], 0, 0)),
            pl.BlockSpec((256, 256), lambda i, te, na: (0, 0)),
        ],
        out_specs=pl.BlockSpec(memory_space=pl.ANY),
        scratch_shapes=[
            pltpu.VMEM((2, MOE_TM * ROW_SUB, LANES), F32),
            pltpu.VMEM((2, MOE_TM * ROW_SUB, LANES), F32),
            pltpu.VMEM((MOE_NBLK, D_MODEL, 256), BF16),
            pltpu.VMEM((D_MODEL, D_MODEL), BF16),
            pltpu.VMEM((MOE_NBLK, MOE_TM, 128), BF16),
            pltpu.SemaphoreType.DMA((2,)),
            pltpu.SemaphoreType.DMA((2,)),
        ],
    )
    return pl.pallas_call(
        _moe_body,
        grid_spec=grid_spec,
        out_shape=jax.ShapeDtypeStruct((Y_ROWS * ROW_SUB, LANES), F32),
        compiler_params=_cp(("arbitrary",), 56 * 1024 * 1024),
        name="moe_experts",
    )(tile_expert, n_active, row_tok, row_tok, row_dst, h_all, wgu, bgu_perm, wd, bd, perm)


INV_GROUP = 16


def _inv_body(gs_ref, cn_ref, na_ref, dest_ref, src_hbm, src_s, sem):
    k = pl.program_id(0)

    def fill(lo, hi):
        def body(p, carry):
            src_s[p] = -1
            return carry
        lax.fori_loop(lo, hi, body, 0)

    @pl.when(k == 0)
    def _():
        end = na_ref[0] * MOE_TM
        for e in range(N_EXPERTS):
            fill(gs_ref[e] + cn_ref[e], gs_ref[e + 1] if e + 1 < N_EXPERTS else end)
        fill(end, P_PAD)

    def body(row, carry):
        val = k * T_ALL + row * LANES
        for c in range(0, LANES, INV_GROUP):
            dests = [dest_ref[row, c + u] for u in range(INV_GROUP)]
            for u in range(INV_GROUP):
                src_s[dests[u]] = val + (c + u)
        return carry
    lax.fori_loop(0, T_ALL // LANES, body, 0)

    @pl.when(k == TOP_K - 1)
    def _():
        cp = pltpu.make_async_copy(src_s, src_hbm, sem)
        cp.start()
        cp.wait()


def _inv_call(gstart, counts, n_active, dest):
    grid_spec = pltpu.PrefetchScalarGridSpec(
        num_scalar_prefetch=3,
        grid=(TOP_K,),
        in_specs=[pl.BlockSpec((None, T_ALL // LANES, LANES), lambda k, *_: (k, 0, 0), memory_space=pltpu.SMEM)],
        out_specs=pl.BlockSpec(memory_space=pl.ANY),
        scratch_shapes=[pltpu.SMEM((P_PAD,), jnp.int32), pltpu.SemaphoreType.DMA(())],
    )
    return pl.pallas_call(
        _inv_body,
        grid_spec=grid_spec,
        out_shape=jax.ShapeDtypeStruct((P_PAD,), jnp.int32),
        compiler_params=_cp(("arbitrary",)),
        name="route_inverse",
    )(gstart, counts, n_active, dest.reshape(TOP_K, T_ALL // LANES, LANES))


def _route(idx_t, rank_t, counts_f):
    counts = counts_f.reshape(N_EXPERTS).astype(jnp.int32)
    tiles = (counts + MOE_TM - 1) // MOE_TM
    tile_end = jnp.cumsum(tiles)
    gstart = (tile_end - tiles) * MOE_TM
    n_active = tile_end[N_EXPERTS - 1]
    ti = jnp.arange(MOE_TILES, dtype=jnp.int32)
    te = jnp.sum(tile_end[None, :] <= jnp.minimum(ti, n_active - 1)[:, None], axis=1, dtype=jnp.int32)
    te = jnp.minimum(te, N_EXPERTS - 1)
    eids = jnp.arange(N_EXPERTS, dtype=jnp.int32)
    dest = rank_t + jnp.sum(jnp.where(idx_t[:, :, None] == eids, gstart, 0), axis=-1, dtype=jnp.int32)
    n_active = n_active.reshape(1)
    src = _inv_call(gstart, counts, n_active, dest).reshape(P_PAD)
    valid = src >= 0
    k = src // T_ALL
    t = src - k * T_ALL
    p = jnp.arange(P_PAD, dtype=jnp.int32)
    dump = TOP_K * T_PAD + ((p // MOE_TM) % 2) * MOE_TM + p % MOE_TM
    row_tok = (jnp.where(valid, t, 0) * ROW_SUB).reshape(MOE_TILES, 1, MOE_TM)
    row_dst = (jnp.where(valid, k * T_PAD + t, dump) * ROW_SUB).reshape(MOE_TILES, 1, MOE_TM)
    first = ((TOP_K * T_PAD + MOE_TM + jnp.arange(MOE_TM, dtype=jnp.int32)) * ROW_SUB).reshape(1, 1, MOE_TM)
    return te, n_active, row_tok, jnp.concatenate([row_dst, first], axis=0)


def _weighted(p_ref, ys):
    acc = p_ref[:, 0:1] * _load_rows(ys[0])
    for k in range(1, TOP_K):
        acc = acc + p_ref[:, k:k + 1] * _load_rows(ys[k])
    return acc


def _combine_body(x_ref, g_ref, p_ref, y0, y1, y2, y3, o_ref):
    o_ref[...] = x_ref[...] + g_ref[...] * _weighted(p_ref, (y0, y1, y2, y3))


def _combine_norm_body(x_ref, g_ref, p_ref, y0, y1, y2, y3, nw_ref, o_ref):
    x = x_ref[...] + g_ref[...] * _weighted(p_ref, (y0, y1, y2, y3))
    o_ref[...] = x * lax.rsqrt(jnp.mean(x * x, axis=-1, keepdims=True) + EPS) * nw_ref[...]


def _combine_nm_body(x_ref, g_ref, p_ref, y0, y1, y2, y3, nw_ref, sc_ref, sh_ref, w_ref, x2_ref, o_ref):
    x2 = x_ref[...] + g_ref[...] * _weighted(p_ref, (y0, y1, y2, y3))
    x2_ref[...] = x2
    h = _normmod(x2, nw_ref[...], sc_ref[...], sh_ref[...]).astype(BF16)
    for s, n in _col_chunks(o_ref.shape[1]):
        o_ref[:, s:s + n] = jnp.dot(h, w_ref[:, s:s + n], preferred_element_type=F32)


def _combine_inproj(x, mod, layer, prr, y_all, tok0, tm, nw, w_bf, name):
    t = x.shape[0]
    n = w_bf.shape[1]
    rows = mod.shape[2]
    tps = (t // mod.shape[1]) // tm
    yspec = lambda k: pl.BlockSpec((tm * ROW_SUB, LANES), lambda i: ((k * T_PAD + tok0) // tm + i, 0))
    return pl.pallas_call(
        _combine_nm_body,
        grid=(t // tm,),
        in_specs=[pl.BlockSpec((tm, D_MODEL), lambda i: (i, 0)), _mod_spec(rows, layer, 5, tps),
                  pl.BlockSpec((tm, LANES), lambda i: (i, 0)), yspec(0), yspec(1), yspec(2), yspec(3),
                  _resident((1, D_MODEL)), _mod_spec(rows, layer + 1, 1, tps), _mod_spec(rows, layer + 1, 0, tps),
                  _resident(w_bf.shape)],
        out_specs=[pl.BlockSpec((tm, D_MODEL), lambda i: (i, 0)), pl.BlockSpec((tm, n), lambda i: (i, 0))],
        out_shape=[jax.ShapeDtypeStruct((t, D_MODEL), F32), jax.ShapeDtypeStruct((t, n), F32)],
        compiler_params=_cp(("arbitrary",)),
        name=name,
    )(x, mod, prr, y_all, y_all, y_all, y_all, nw, mod, mod, w_bf)


def _combine(x, mod, layer, prr, y_all, tok0, tm, final_w=None, name="combine"):
    t = x.shape[0]
    rows = mod.shape[2]
    tps = (t // mod.shape[1]) // tm
    yspec = lambda k: pl.BlockSpec((tm * ROW_SUB, LANES), lambda i: ((k * T_PAD + tok0) // tm + i, 0))
    in_specs = [pl.BlockSpec((tm, D_MODEL), lambda i: (i, 0)), _mod_spec(rows, layer, 5, tps),
                pl.BlockSpec((tm, LANES), lambda i: (i, 0)), yspec(0), yspec(1), yspec(2), yspec(3)]
    args = [x, mod, prr, y_all, y_all, y_all, y_all]
    body = _combine_body
    if final_w is not None:
        in_specs.append(_resident((1, D_MODEL)))
        args.append(final_w)
        body = _combine_norm_body
    return pl.pallas_call(
        body,
        grid=(t // tm,),
        in_specs=in_specs,
        out_specs=pl.BlockSpec((tm, D_MODEL), lambda i: (i, 0)),
        out_shape=jax.ShapeDtypeStruct((t, D_MODEL), F32),
        compiler_params=_cp(("arbitrary",)),
        name=name,
    )(*args)


def _mixer_out_and_moe(layer, x_p, a_p, x_s, a_s, w_out, mod_p, mod_s, nw, rwt, rb, tm_p, tm_s,
                       moe_w_gate_up, bgu_perm, moe_w_down, moe_b_down, perm):
    zero_cnt = jnp.zeros((N_EXPERTS, 1), F32)
    x1_p, h_all, idx_p, rank_p, prr_p, cnt = _outproj(x_p, a_p, w_out, mod_p, layer, nw, rwt, rb, tm_p, zero_cnt,
                                                      None, "outproj%d_prompt" % layer)
    x1_s, h_all, idx_s, rank_s, prr_s, cnt = _outproj(x_s, a_s, w_out, mod_s, layer, nw, rwt, rb, tm_s, cnt,
                                                      h_all, "outproj%d_sample" % layer)
    idx_t = jnp.concatenate([idx_p, idx_s], axis=1)
    rank_t = jnp.concatenate([rank_p, rank_s], axis=1)
    te, n_active, row_tok, row_dst = _route(idx_t, rank_t, cnt)
    y_all = _moe_call(layer, h_all, te, n_active, row_tok, row_dst, moe_w_gate_up, bgu_perm,
                      moe_w_down, moe_b_down.reshape(-1, N_EXPERTS, 1, D_MODEL), perm)
    return x1_p, x1_s, prr_p, prr_s, y_all


def kernel(x_prompt, x_sample, c_prompt, c_sample, state_ssm, state_ssm_conv, state_sconv, ada_w, ada_b, norm1_w, norm2_w, ssd_w_in, ssd_conv_w, ssd_conv_b, ssd_dt_bias, ssd_A_log, ssd_D, ssd_norm_w, ssd_w_out, sc_w_in, sc_conv_w, sc_w_out, router_w, router_b, moe_w_gate_up, moe_b_gate_up, moe_w_down, moe_b_down, final_norm_w):
    tm_p = 256
    tm_s = DEC_BATCH
    depth = ada_w.shape[0]

    def hilo(w):
        hi = lax.bitcast_convert_type(lax.bitcast_convert_type(w, jnp.uint32) & jnp.uint32(0xFFFF0000), F32)
        return jnp.stack([hi.astype(BF16), (w - hi).astype(BF16)])

    w_in0 = hilo(ssd_w_in)
    w_out0 = hilo(ssd_w_out)
    w_in1 = sc_w_in.astype(BF16)
    w_out1 = sc_w_out.astype(BF16)
    pad_h = lambda v: jnp.concatenate([v, jnp.zeros((HEADS_PAD - SSD_HEADS,), F32)]).reshape(1, HEADS_PAD)
    dtb = pad_h(ssd_dt_bias)
    alog = pad_h(ssd_A_log)
    dch = jnp.repeat(ssd_D, SSD_HEAD_DIM).reshape(1, SSD_INNER)
    ssd_nw = ssd_norm_w.reshape(1, SSD_INNER)
    conv_b = ssd_conv_b.reshape(1, SSD_CONV_DIM)
    emat = (jnp.arange(SSD_INNER, dtype=jnp.int32)[None, :] // SSD_HEAD_DIM
            == jnp.arange(HEADS_PAD, dtype=jnp.int32)[:, None]).astype(BF16)
    jj = jnp.arange(256, dtype=jnp.int32)
    perm = (jj[None, :] == jnp.where(jj % 2 == 0, jj // 2, 128 + jj // 2)[:, None]).astype(BF16)
    bgu_perm = moe_b_gate_up.reshape(depth, N_EXPERTS, 8, 128, 2).transpose(0, 1, 2, 4, 3).reshape(
        depth, N_EXPERTS, MOE_NBLK, 256)
    rwt = jnp.transpose(router_w, (0, 2, 1))
    rb = router_b.reshape(depth, N_EXPERTS, 1)
    n1 = norm1_w.reshape(depth, 1, D_MODEL)
    n2 = norm2_w.reshape(depth, 1, D_MODEL)

    mod = _ada_call(jnp.concatenate([c_prompt, c_sample], axis=0), ada_w, ada_b)
    mod_p = mod[:, :BATCH].reshape(depth, BATCH, 1, 6 * D_MODEL)
    mod_s = mod[:, BATCH:].reshape(depth, 1, DEC_BATCH, 6 * D_MODEL)

    xp = x_prompt.reshape(T_PROMPT, D_MODEL)
    xs = x_sample.reshape(DEC_BATCH, D_MODEL)

    zx_p = _normmod_matmul(xp, n1[0], mod_p, 0, 1, 0, w_in0, tm_p, "inproj0_prompt")
    gn_p, ssm_p, ssm_conv_p = _ssd_prompt(zx_p, ssd_conv_w, conv_b, dtb, alog, dch, ssd_nw, emat)
    zx_s = _normmod_matmul(xs, n1[0], mod_s, 0, 1, 0, w_in0, tm_s, "inproj0_sample")
    xc_s, xdt_s, dec_s, ncs_s = _ssd_sample_prep(
        zx_s, state_ssm_conv.reshape(DEC_BATCH, (SSD_CONV - 1) * SSD_CONV_DIM), ssd_conv_w, conv_b, dtb, alog, emat)
    steps = DEC_BATCH // SAMPLE_BS
    to_cols = lambda a: a.reshape(steps, SAMPLE_BS, SSD_INNER).transpose(0, 2, 1)
    b3 = xc_s[:, SSD_INNER:SSD_INNER + SSD_GN].reshape(DEC_BATCH, SSD_GROUPS, SSD_STATE)
    c3 = xc_s[:, SSD_INNER + SSD_GN:].reshape(DEC_BATCH, SSD_GROUPS, SSD_STATE).transpose(1, 0, 2)
    ssm_s, gn_s = _ssd_sample_state(state_ssm, to_cols(xdt_s), dec_s, b3, c3, xc_s, zx_s, dch, ssd_nw)
    ssm_conv_s = ncs_s.reshape(DEC_BATCH, SSD_CONV - 1, SSD_CONV_DIM)

    x1_p, x1_s, prr_p, prr_s, y0 = _mixer_out_and_moe(
        0, xp, gn_p, xs, gn_s, w_out0, mod_p, mod_s, n2[0], rwt[0], rb[0], tm_p, tm_s,
        moe_w_gate_up, bgu_perm, moe_w_down, moe_b_down, perm)
    x2_p, bcx_p = _combine_inproj(x1_p, mod_p, 0, prr_p, y0, 0, tm_p, n1[1], w_in1, "combine0_inproj1_prompt")
    x2_s, bcx_s = _combine_inproj(x1_s, mod_s, 0, prr_s, y0, T_PROMPT, tm_s, n1[1], w_in1,
                                  "combine0_inproj1_sample")
    v_p, sconv_p = _sconv_prompt(bcx_p, sc_conv_w)
    v_s, nsc_s = _sconv_sample(bcx_s, state_sconv.reshape(DEC_BATCH, (SC_WIDTH - 1) * D_MODEL), sc_conv_w)
    sconv_s = nsc_s.reshape(DEC_BATCH, SC_WIDTH - 1, D_MODEL)

    x3_p, x3_s, prr_p, prr_s, y1 = _mixer_out_and_moe(
        1, x2_p, v_p, x2_s, v_s, w_out1, mod_p, mod_s, n2[1], rwt[1], rb[1], tm_p, tm_s,
        moe_w_gate_up, bgu_perm, moe_w_down, moe_b_down, perm)
    fw = final_norm_w.reshape(1, D_MODEL)
    y_p = _combine(x3_p, mod_p, 1, prr_p, y1, 0, tm_p, final_w=fw, name="final_prompt")
    y_s = _combine(x3_s, mod_s, 1, prr_s, y1, T_PROMPT, tm_s, final_w=fw, name="final_sample")

    return (y_p.reshape(BATCH, SEQ, D_MODEL), y_s.reshape(DEC_BATCH, 1, D_MODEL), ssm_p, ssm_conv_p, sconv_p,
            ssm_s, ssm_conv_s, sconv_s)
```
